```python
import math
import jax
import jax.numpy as jnp
from jax import lax
import numpy as np

D_MODEL = 1024
BATCH = 16
SEQ = 256
DEPTH = 4
DEC_BATCH = 8
DEC_SEQ = 1024
PAST_LEN = 256

GRID_W = 64
ATTN_WIDTH = 512
N_HEADS = 4
HEAD_DIM = 64
CONV_WIDTH = 512
CONV_KERNEL = 31
N_BRANCH = 2
N_EXPERTS = 32
TOP_K = 4
D_EXPERT = 1024
SWIGLU_LIMIT = 7.0
SWIGLU_ALPHA = 1.702
ROPE_BASE = 10000.0
Q_BLOCK = 128
MOE_BLOCK = 128
EPS = 1e-6
IN_WIDTH = 3 * ATTN_WIDTH + 2 * CONV_WIDTH + N_BRANCH * D_MODEL

kernel_name = "prefix_diffattn_conformer_moe_denoise_step"


def rms_norm(x, g):
    xf = x.astype(jnp.float32)
    y = xf * lax.rsqrt(jnp.mean(xf * xf, axis=-1, keepdims=True) + EPS)
    return (y * g.astype(jnp.float32)).astype(x.dtype)


def layer_norm(x, g, b):
    xf = x.astype(jnp.float32)
    mu = jnp.mean(xf, axis=-1, keepdims=True)
    var = jnp.mean(jnp.square(xf - mu), axis=-1, keepdims=True)
    y = (xf - mu) * lax.rsqrt(var + EPS)
    return (y * g.astype(jnp.float32) + b.astype(jnp.float32)).astype(x.dtype)


def axial_rope_tables(n_tokens):
    rows = n_tokens // GRID_W
    row = jnp.repeat(jnp.arange(rows, dtype=jnp.float32), GRID_W)
    col = jnp.tile(jnp.arange(GRID_W, dtype=jnp.float32), rows)
    n_freq = HEAD_DIM // 4
    inv = ROPE_BASE ** (-jnp.arange(n_freq, dtype=jnp.float32) / n_freq)
    ang = jnp.stack([row[:, None] * inv, col[:, None] * inv], axis=1)
    return jnp.cos(ang), jnp.sin(ang)


def apply_axial_rope(x, cos, sin):
    B, S = x.shape[:2]
    xf = x.astype(jnp.float32).reshape(B, S, N_HEADS, 2, 2, 2, HEAD_DIM // 4)
    xa, xb = xf[..., 0, :], xf[..., 1, :]
    c = cos[None, :, None, None]
    s = sin[None, :, None, None]
    out = jnp.stack([xa * c - xb * s, xb * c + xa * s], axis=-2)
    return out.reshape(x.shape).astype(x.dtype)


def diff_attention(q, k, v, lam):
    B, Sq = q.shape[:2]
    n_blk = Sq // Q_BLOCK
    qb = jnp.moveaxis(q.reshape(B, n_blk, Q_BLOCK, N_HEADS, 2, HEAD_DIM), 1, 0)
    scale = HEAD_DIM ** -0.5

    def block(q_blk):
        s = jnp.einsum('bqhmd,bkhmd->bhmqk', q_blk, k).astype(jnp.float32) * scale
        p = jax.nn.softmax(s, axis=-1)
        w = p[:, :, 0] - lam * p[:, :, 1]
        return jnp.einsum('bhqk,bkhe->bqhe', w.astype(v.dtype), v)

    out = lax.map(block, qb)
    return jnp.moveaxis(out, 0, 1).reshape(B, Sq, N_HEADS, 2 * HEAD_DIM)


def depthwise_conv(x, w, b):
    y = lax.conv_general_dilated(
        x, w[:, None, :], window_strides=(1,),
        padding=[(CONV_KERNEL // 2, CONV_KERNEL // 2)],
        dimension_numbers=('NWC', 'WIO', 'NWC'),
        feature_group_count=CONV_WIDTH)
    return y + b


def token_mixers(h, lam_init, w_in, q_norm, k_norm, lam_qk, sub_norm, w_attn_out,
                 conv_w, conv_b, conv_norm_g, conv_norm_b, w_conv_out, w_out,
                 rope=None, ctx_k=None, ctx_v=None):
    B, S, _ = h.shape
    proj = h @ w_in
    q, k, v, u, gl = jnp.split(
        proj, [ATTN_WIDTH, 2 * ATTN_WIDTH, 3 * ATTN_WIDTH, 3 * ATTN_WIDTH + 2 * CONV_WIDTH], axis=-1)
    q = rms_norm(q.reshape(B, S, N_HEADS, 2, HEAD_DIM), q_norm)
    k = rms_norm(k.reshape(B, S, N_HEADS, 2, HEAD_DIM), k_norm)
    v = v.reshape(B, S, N_HEADS, 2 * HEAD_DIM)
    if rope is not None:
        q = apply_axial_rope(q, rope[0], rope[1])
        k = apply_axial_rope(k, rope[0], rope[1])
    k_own, v_own = k, v
    if ctx_k is not None:
        k = jnp.concatenate([ctx_k, k], axis=1)
        v = jnp.concatenate([ctx_v, v], axis=1)
    lf = lam_qk.astype(jnp.float32)
    lam = jnp.exp(jnp.sum(lf[0] * lf[1])) - jnp.exp(jnp.sum(lf[2] * lf[3])) + lam_init
    o = diff_attention(q, k, v, lam)
    o = rms_norm(o, sub_norm) * (1.0 - lam_init)
    attn_d = o.reshape(B, S, ATTN_WIDTH) @ w_attn_out
    a, g = jnp.split(u, 2, axis=-1)
    z = a * jax.nn.sigmoid(g)
    z = depthwise_conv(z, conv_w, conv_b)
    z = jax.nn.silu(layer_norm(z, conv_norm_g, conv_norm_b))
    conv_d = z @ w_conv_out
    gates = jax.nn.sigmoid(gl).reshape(B, S, N_BRANCH, D_MODEL)
    y = gates[:, :, 0] * attn_d + gates[:, :, 1] * conv_d
    return y @ w_out, k_own, v_own


def moe(x, w_router, b_router, w_gate_up, b_gate_up, w_down, b_down):
    B, S, D = x.shape
    n_tok = B * S
    xt = x.reshape(n_tok, D)
    logits = (xt @ w_router + b_router).astype(jnp.float32)
    top_logit, top_e = lax.top_k(logits, TOP_K)
    gate = jax.nn.softmax(top_logit, axis=-1)
    n_assign = n_tok * TOP_K
    flat_e = top_e.reshape(n_assign)
    flat_tok = jnp.repeat(jnp.arange(n_tok, dtype=jnp.int32), TOP_K)
    order = jnp.argsort(flat_e)
    sorted_e = flat_e[order]
    sorted_tok = flat_tok[order]
    sorted_gate = gate.reshape(n_assign)[order]
    counts = jnp.bincount(flat_e, length=N_EXPERTS)
    padded = (counts + MOE_BLOCK - 1) // MOE_BLOCK * MOE_BLOCK
    start = jnp.cumsum(counts) - counts
    padded_end = jnp.cumsum(padded)
    padded_start = padded_end - padded
    dest = padded_start[sorted_e] + (jnp.arange(n_assign, dtype=jnp.int32) - start[sorted_e])
    n_blocks = (n_assign + N_EXPERTS * (MOE_BLOCK - 1) + MOE_BLOCK - 1) // MOE_BLOCK
    rows_tok = jnp.zeros((n_blocks * MOE_BLOCK,), jnp.int32).at[dest].set(sorted_tok)
    block_start = jnp.arange(n_blocks, dtype=jnp.int32) * MOE_BLOCK
    block_e = jnp.minimum(jnp.searchsorted(padded_end, block_start, side='right'), N_EXPERTS - 1)
    xb = xt[rows_tok].reshape(n_blocks, MOE_BLOCK, D)

    def expert_block(args):
        x_blk, e = args
        gu = x_blk @ w_gate_up[e] + b_gate_up[e]
        g, u = jnp.split(gu, 2, axis=-1)
        g = jnp.minimum(g, SWIGLU_LIMIT)
        u = jnp.clip(u, -SWIGLU_LIMIT, SWIGLU_LIMIT)
        act = (u + 1.0) * (g * jax.nn.sigmoid(SWIGLU_ALPHA * g))
        return act @ w_down[e] + b_down[e]

    yb = lax.map(expert_block, (xb, block_e))
    y_assign = yb.reshape(n_blocks * MOE_BLOCK, D)[dest].astype(jnp.float32) * sorted_gate[:, None]
    y = jax.ops.segment_sum(y_assign, sorted_tok, num_segments=n_tok)
    return y.astype(x.dtype).reshape(B, S, D)


def ada_modulation(cond, w, b):
    mod = jax.nn.silu(cond) @ w + b
    return [m[:, None, :] for m in jnp.split(mod, 6, axis=-1)]


def modulate(h, shift, scale):
    return h * (1.0 + scale) + shift


def setup_inputs(seed: int = 0) -> dict:
    key = jax.random.key(seed)
    ks = jax.random.split(key, 32)
    f32 = jnp.float32
    L = DEPTH

    def nrm(k, shape, scale):
        return jax.random.normal(k, shape, f32) * scale

    cache_shape = (DEC_BATCH, DEPTH, PAST_LEN, N_HEADS, 2 * HEAD_DIM)
    return {
        'x_prompt': nrm(ks[0], (BATCH, SEQ, D_MODEL), 1.0),
        'x_sample': nrm(ks[1], (DEC_BATCH, DEC_SEQ, D_MODEL), 1.0),
        'c': nrm(ks[2], (DEC_BATCH, D_MODEL), 1.0),
        'cache_k': nrm(ks[3], cache_shape, 1.0),
        'cache_v': nrm(ks[4], cache_shape, 1.0),
        'c_ctx': nrm(ks[5], (D_MODEL,), 1.0),
        'w_ada': nrm(ks[6], (L, D_MODEL, 6 * D_MODEL), 0.5 * D_MODEL ** -0.5),
        'b_ada': nrm(ks[7], (L, 6 * D_MODEL), 0.02),
        'norm_mix': 1.0 + nrm(ks[8], (L, D_MODEL), 0.02),
        'norm_ffn': 1.0 + nrm(ks[9], (L, D_MODEL), 0.02),
        'w_in': nrm(ks[10], (L, D_MODEL, IN_WIDTH), D_MODEL ** -0.5),
        'q_norm': 1.0 + nrm(ks[11], (L, HEAD_DIM), 0.02),
        'k_norm': 1.0 + nrm(ks[12], (L, HEAD_DIM), 0.02),
        'lam_qk': nrm(ks[13], (L, 4, HEAD_DIM), 0.1),
        'sub_norm': 1.0 + nrm(ks[14], (L, 2 * HEAD_DIM), 0.02),
        'w_attn_out': nrm(ks[15], (L, ATTN_WIDTH, D_MODEL), ATTN_WIDTH ** -0.5),
        'conv_w': nrm(ks[16], (L, CONV_KERNEL, CONV_WIDTH), CONV_KERNEL ** -0.5),
        'conv_b': nrm(ks[17], (L, CONV_WIDTH), 0.01),
        'conv_norm_g': 1.0 + nrm(ks[18], (L, CONV_WIDTH), 0.02),
        'conv_norm_b': nrm(ks[19], (L, CONV_WIDTH), 0.01),
        'w_conv_out': nrm(ks[20], (L, CONV_WIDTH, D_MODEL), CONV_WIDTH ** -0.5),
        'w_out': nrm(ks[21], (L, D_MODEL, D_MODEL), D_MODEL ** -0.5),
        'w_router': nrm(ks[22], (L, D_MODEL, N_EXPERTS), D_MODEL ** -0.5),
        'b_router': nrm(ks[23], (L, N_EXPERTS), 0.01),
        'w_gate_up': nrm(ks[24], (L, N_EXPERTS, D_MODEL, 2 * D_EXPERT), D_MODEL ** -0.5),
        'b_gate_up': nrm(ks[25], (L, N_EXPERTS, 2 * D_EXPERT), 0.01),
        'w_down': nrm(ks[26], (L, N_EXPERTS, D_EXPERT, D_MODEL), D_EXPERT ** -0.5),
        'b_down': nrm(ks[27], (L, N_EXPERTS, D_MODEL), 0.01),
    }


def reference(x_prompt, x_sample, c, cache_k, cache_v, c_ctx, w_ada, b_ada, norm_mix, norm_ffn,
              w_in, q_norm, k_norm, lam_qk, sub_norm, w_attn_out, conv_w, conv_b, conv_norm_g,
              conv_norm_b, w_conv_out, w_out, w_router, b_router, w_gate_up, b_gate_up,
              w_down, b_down):
    b_p, s_p = x_prompt.shape[0], x_prompt.shape[1]
    b_s, past = cache_k.shape[0], cache_k.shape[2]
    rope = axial_rope_tables(x_sample.shape[1])
    ctx_cond = jnp.broadcast_to(c_ctx, (b_p, D_MODEL))
    xp, xs = x_prompt, x_sample
    new_k, new_v = [], []
    for l in range(DEPTH):
        lam_init = 0.8 - 0.6 * math.exp(-0.3 * l)
        mix_params = (w_in[l], q_norm[l], k_norm[l], lam_qk[l], sub_norm[l], w_attn_out[l],
                      conv_w[l], conv_b[l], conv_norm_g[l], conv_norm_b[l], w_conv_out[l], w_out[l])
        moe_params = (w_router[l], b_router[l], w_gate_up[l], b_gate_up[l], w_down[l], b_down[l])
        sh1, sc1, g1, sh2, sc2, g2 = ada_modulation(ctx_cond, w_ada[l], b_ada[l])
        h = modulate(rms_norm(xp, norm_mix[l]), sh1, sc1)
        m, kc, vc = token_mixers(h, lam_init, *mix_params)
        xp = xp + g1 * m
        h = modulate(rms_norm(xp, norm_ffn[l]), sh2, sc2)
        xp = xp + g2 * moe(h, *moe_params)
        new_k.append(kc.reshape(b_p, s_p, N_HEADS, 2 * HEAD_DIM))
        new_v.append(vc)
        sh1, sc1, g1, sh2, sc2, g2 = ada_modulation(c, w_ada[l], b_ada[l])
        h = modulate(rms_norm(xs, norm_mix[l]), sh1, sc1)
        ctx_k = cache_k[:, l].reshape(b_s, past, N_HEADS, 2, HEAD_DIM)
        ctx_v = cache_v[:, l]
        m, _, _ = token_mixers(h, lam_init, *mix_params, rope=rope, ctx_k=ctx_k, ctx_v=ctx_v)
        xs = xs + g1 * m
        h = modulate(rms_norm(xs, norm_ffn[l]), sh2, sc2)
        xs = xs + g2 * moe(h, *moe_params)
    new_cache_k = jnp.stack(new_k, axis=1)
    new_cache_v = jnp.stack(new_v, axis=1)
    return (xp, xs, new_cache_k, new_cache_v)
```

```python
import functools
import math

import jax
import jax.numpy as jnp
from jax import lax
from jax.experimental import pallas as pl
from jax.experimental.pallas import tpu as pltpu

F32 = jnp.float32
BF16 = jnp.bfloat16

D_MODEL = 1024
DEPTH = 4
GRID_W = 64
ATTN_WIDTH = 512
N_HEADS = 4
HEAD_DIM = 64
CONV_WIDTH = 512
CONV_KERNEL = 31
N_EXPERTS = 32
TOP_K = 4
D_EXPERT = 1024
SWIGLU_LIMIT = 7.0
SWIGLU_ALPHA = 1.702
ROPE_BASE = 10000.0
EPS = 1e-6

LANES = 128
TM = 256
EBLK = 256
ADA_TN = 1024
CONV_CHUNK = 64
CONV_HALO = 16
NEG_BIG = -1e30
VMEM_LIMIT = 56 * 1024 * 1024


def _cparams(n_axes, vmem=None):
    return pltpu.CompilerParams(
        dimension_semantics=("arbitrary",) * n_axes,
        vmem_limit_bytes=VMEM_LIMIT if vmem is None else vmem)


def _mod_row_of_tile(i, n_ctx_tiles, tiles_per_seq):
    return jnp.where(i < n_ctx_tiles, 0, 1 + (i - n_ctx_tiles) // tiles_per_seq)


def _ada_kernel(cond_ref, w_ref, b_ref, o_ref):
    cnd = cond_ref[...]
    act = cnd * jax.nn.sigmoid(cnd)
    o_ref[0] = jnp.dot(act.astype(BF16), w_ref[0], preferred_element_type=F32) + b_ref[0]


def _ada_all_layers(cond, w_ada_bf, b_ada):
    n_rows = cond.shape[0]
    n_col = w_ada_bf.shape[2]
    return pl.pallas_call(
        _ada_kernel,
        grid=(DEPTH, n_col // ADA_TN),
        in_specs=[
            pl.BlockSpec((n_rows, D_MODEL), lambda l, j: (0, 0)),
            pl.BlockSpec((1, D_MODEL, ADA_TN), lambda l, j: (l, 0, j)),
            pl.BlockSpec((1, 1, ADA_TN), lambda l, j: (l, 0, j)),
        ],
        out_specs=pl.BlockSpec((1, n_rows, ADA_TN), lambda l, j: (l, 0, j)),
        out_shape=jax.ShapeDtypeStruct((DEPTH, n_rows, n_col), F32),
        compiler_params=_cparams(2),
        name="ada",
    )(cond, w_ada_bf, b_ada.reshape(DEPTH, 1, n_col))


def _rms_mod(x, g, shift, scale):
    y = x * lax.rsqrt(jnp.mean(x * x, axis=-1, keepdims=True) + EPS) * g
    return y * (1.0 + scale) + shift


def _group_mean_sq(t, gsum):
    s = t * t
    hi = s.astype(BF16)
    lo = (s - hi.astype(F32)).astype(BF16)
    tot = jnp.dot(hi, gsum, preferred_element_type=F32) + jnp.dot(lo, gsum, preferred_element_type=F32)
    return tot * (1.0 / HEAD_DIM)


def _rope_slab(x, cos, sgn_sin, first_half):
    partner = jnp.where(first_half, pltpu.roll(x, LANES - 16, 1), pltpu.roll(x, 16, 1))
    return x * cos + partner * sgn_sin


def _inproj_kernel(x_ref, mod_ref, gmix_ref, w_ref, qn_ref, kn_ref, gsum_ref, cos_ref, sin_ref,
                   q_ref, k_ref, v_ref, z_ref, gate_ref):
    m = mod_ref[0]
    h = _rms_mod(x_ref[...], gmix_ref[...], m[0:1], m[1:2]).astype(BF16)
    aw = ATTN_WIDTH
    q = jnp.dot(h, w_ref[:, 0:aw], preferred_element_type=F32)
    k = jnp.dot(h, w_ref[:, aw:2 * aw], preferred_element_type=F32)
    gsum = gsum_ref[...]
    q = q * lax.rsqrt(_group_mean_sq(q, gsum) + EPS) * qn_ref[...]
    k = k * lax.rsqrt(_group_mean_sq(k, gsum) + EPS) * kn_ref[...]
    cos = cos_ref[...]
    sin = sin_ref[...]
    lane = lax.broadcasted_iota(jnp.int32, (TM, LANES), 1)
    first_half = (lane & 16) == 0
    for hd in range(N_HEADS):
        sl = slice(hd * LANES, (hd + 1) * LANES)
        q_ref[:, sl] = (_rope_slab(q[:, sl], cos, sin, first_half) * (HEAD_DIM ** -0.5)).astype(BF16)
        k_ref[:, sl] = _rope_slab(k[:, sl], cos, sin, first_half)
    v_ref[...] = jnp.dot(h, w_ref[:, 2 * aw:3 * aw], preferred_element_type=F32)
    c0 = 3 * aw
    a = jnp.dot(h, w_ref[:, c0:c0 + CONV_WIDTH], preferred_element_type=F32)
    g = jnp.dot(h, w_ref[:, c0 + CONV_WIDTH:c0 + 2 * CONV_WIDTH], preferred_element_type=F32)
    z_ref[...] = a * jax.nn.sigmoid(g)
    c1 = c0 + 2 * CONV_WIDTH
    gl = jnp.dot(h, w_ref[:, c1:c1 + 2 * D_MODEL], preferred_element_type=F32)
    gate_ref[...] = jax.nn.sigmoid(gl).astype(BF16)


def _inproj(x, mod_l, gmix, w_in_bf, l, qn, kn, gsum, cos_t, sin_t, n_ctx_tiles, tiles_per_seq):
    n_tok = x.shape[0]
    in_w = w_in_bf.shape[2]
    row = lambda i: (i, 0)
    modrow = lambda i: (_mod_row_of_tile(i, n_ctx_tiles, tiles_per_seq), 0, 0)
    ropeblk = lambda i: (jnp.where(i < n_ctx_tiles, 0, 1 + (i - n_ctx_tiles) % tiles_per_seq), 0)
    const2 = lambda i: (0, 0)
    return pl.pallas_call(
        _inproj_kernel,
        grid=(n_tok // TM,),
        in_specs=[
            pl.BlockSpec((TM, D_MODEL), row),
            pl.BlockSpec((1, 6, D_MODEL), modrow),
            pl.BlockSpec((1, D_MODEL), const2),
            pl.BlockSpec((None, D_MODEL, in_w), lambda i: (l, 0, 0)),
            pl.BlockSpec((1, ATTN_WIDTH), const2),
            pl.BlockSpec((1, ATTN_WIDTH), const2),
            pl.BlockSpec((ATTN_WIDTH, ATTN_WIDTH), const2),
            pl.BlockSpec((TM, LANES), ropeblk),
            pl.BlockSpec((TM, LANES), ropeblk),
        ],
        out_specs=[
            pl.BlockSpec((TM, ATTN_WIDTH), row),
            pl.BlockSpec((TM, ATTN_WIDTH), row),
            pl.BlockSpec((TM, ATTN_WIDTH), row),
            pl.BlockSpec((TM, CONV_WIDTH), row),
            pl.BlockSpec((TM, 2 * D_MODEL), row),
        ],
        out_shape=[
            jax.ShapeDtypeStruct((n_tok, ATTN_WIDTH), BF16),
            jax.ShapeDtypeStruct((n_tok, ATTN_WIDTH), F32),
            jax.ShapeDtypeStruct((n_tok, ATTN_WIDTH), F32),
            jax.ShapeDtypeStruct((n_tok, CONV_WIDTH), F32),
            jax.ShapeDtypeStruct((n_tok, 2 * D_MODEL), BF16),
        ],
        compiler_params=_cparams(1),
        name="inproj",
    )(x, mod_l, gmix, w_in_bf, qn, kn, gsum, cos_t, sin_t)


def _attn_kernel(*refs, lam_init, s_q, s_ctx, q_chunk):
    if s_ctx:
        q_ref, k_ref, v_ref, ck_ref, cv_ref, lam_ref, sub_ref, o_ref, kc_ref, vc_ref = refs
        kc_ref[0:s_ctx, :] = ck_ref[...].astype(BF16)
        vc_ref[0:s_ctx, :] = cv_ref[...].astype(BF16)
    else:
        q_ref, k_ref, v_ref, lam_ref, sub_ref, o_ref, kc_ref, vc_ref = refs
    kc_ref[s_ctx:s_ctx + s_q, :] = k_ref[...].astype(BF16)
    vc_ref[s_ctx:s_ctx + s_q, :] = v_ref[...].astype(BF16)
    lq = lam_ref[...]
    lam = (jnp.exp(jnp.sum(lq[0:1] * lq[1:2], axis=-1, keepdims=True))
           - jnp.exp(jnp.sum(lq[2:3] * lq[3:4], axis=-1, keepdims=True)) + lam_init)
    lane = lax.broadcasted_iota(jnp.int32, (q_chunk, LANES), 1)
    map0 = lane < HEAD_DIM
    sub = sub_ref[...]
    nt = (((1,), (1,)), ((), ()))

    def chunk(c, carry):
        r0 = pl.multiple_of(c * q_chunk, q_chunk)
        q = q_ref[pl.ds(r0, q_chunk), :]
        kc = kc_ref[...]
        zero = jnp.zeros_like(q)
        probs = []
        for mp in range(2):
            qm = jnp.where(map0, q, zero) if mp == 0 else jnp.where(map0, zero, q)
            s = lax.dot_general(qm, kc, nt, preferred_element_type=F32)
            e = jnp.exp(s - jnp.max(s, axis=-1, keepdims=True))
            probs.append(e / jnp.sum(e, axis=-1, keepdims=True))
        w = probs[0] - lam * probs[1]
        o = jnp.dot(w.astype(BF16), vc_ref[...], preferred_element_type=F32)
        o = o * lax.rsqrt(jnp.mean(o * o, axis=-1, keepdims=True) + EPS) * sub * (1.0 - lam_init)
        o_ref[pl.ds(r0, q_chunk), :] = o.astype(BF16)
        return carry

    lax.fori_loop(0, s_q // q_chunk, chunk, 0)


def _attention(q, k, v, lam_qk_l, sub_l, lam_init, n_batch, s_q, row_blk0, ctx=None):
    s_ctx = 0 if ctx is None else ctx[0].shape[2]
    own = pl.BlockSpec((s_q, LANES), lambda b, h: (row_blk0 + b, h))
    in_specs = [own, own, own]
    args = [q, k, v]
    if ctx is not None:
        ck, cv, l = ctx
        cspec = pl.BlockSpec((None, None, s_ctx, LANES), lambda b, h: (b, l, 0, h))
        in_specs += [cspec, cspec]
        args += [ck, cv]
    in_specs += [pl.BlockSpec((4, HEAD_DIM), lambda b, h: (0, 0)),
                 pl.BlockSpec((1, LANES), lambda b, h: (0, 0))]
    args += [lam_qk_l, sub_l]
    return pl.pallas_call(
        functools.partial(_attn_kernel, lam_init=lam_init, s_q=s_q, s_ctx=s_ctx, q_chunk=min(s_q, 256)),
        grid=(n_batch, N_HEADS),
        in_specs=in_specs,
        out_specs=pl.BlockSpec((s_q, LANES), lambda b, h: (b, h)),
        out_shape=jax.ShapeDtypeStruct((n_batch * s_q, ATTN_WIDTH), BF16),
        scratch_shapes=[pltpu.VMEM((s_ctx + s_q, LANES), BF16), pltpu.VMEM((s_ctx + s_q, LANES), BF16)],
        compiler_params=_cparams(2),
        name="attn_ctx" if ctx is not None else "attn",
    )(*args)


def _conv_kernel(z_ref, w_ref, b_ref, g_ref, beta_ref, o_ref, pad_ref, *, s_len):
    halo = jnp.zeros((CONV_HALO, CONV_WIDTH), F32)
    pad_ref[0:CONV_HALO, :] = halo
    pad_ref[CONV_HALO:CONV_HALO + s_len, :] = z_ref[...]
    pad_ref[CONV_HALO + s_len:2 * CONV_HALO + s_len, :] = halo
    bias = b_ref[...]
    gam = g_ref[...]
    beta = beta_ref[...]
    off0 = CONV_HALO - CONV_KERNEL // 2

    def chunk(c, carry):
        r0 = pl.multiple_of(c * CONV_CHUNK, CONV_CHUNK)
        slabs = []
        for s0 in range(0, CONV_WIDTH, LANES):
            sl = slice(s0, s0 + LANES)
            win = pad_ref[pl.ds(r0, CONV_CHUNK + 2 * CONV_HALO), sl]
            part = jnp.broadcast_to(bias[:, sl], (CONV_CHUNK, LANES))
            for res in range(8):
                shifted = win[res:res + CONV_CHUNK + 2 * CONV_HALO - 8]
                for a8 in range(0, 2 * CONV_HALO, 8):
                    j = a8 + res - off0
                    if 0 <= j < CONV_KERNEL:
                        part = part + shifted[a8:a8 + CONV_CHUNK] * w_ref[j:j + 1, sl]
            slabs.append(part)
        acc = jnp.concatenate(slabs, axis=-1)
        mu = jnp.mean(acc, axis=-1, keepdims=True)
        d = acc - mu
        var = jnp.mean(d * d, axis=-1, keepdims=True)
        y = d * lax.rsqrt(var + EPS) * gam + beta
        o_ref[pl.ds(r0, CONV_CHUNK), :] = (y * jax.nn.sigmoid(y)).astype(BF16)
        return carry

    lax.fori_loop(0, s_len // CONV_CHUNK, chunk, 0)


def _conv_module(z, conv_w_l, conv_b_l, g_l, beta_l, n_batch, s_len, row_blk0):
    const2 = lambda b: (0, 0)
    return pl.pallas_call(
        functools.partial(_conv_kernel, s_len=s_len),
        grid=(n_batch,),
        in_specs=[
            pl.BlockSpec((s_len, CONV_WIDTH), lambda b: (row_blk0 + b, 0)),
            pl.BlockSpec((CONV_KERNEL, CONV_WIDTH), const2),
            pl.BlockSpec((1, CONV_WIDTH), const2),
            pl.BlockSpec((1, CONV_WIDTH), const2),
            pl.BlockSpec((1, CONV_WIDTH), const2),
        ],
        out_specs=pl.BlockSpec((s_len, CONV_WIDTH), lambda b: (b, 0)),
        out_shape=jax.ShapeDtypeStruct((n_batch * s_len, CONV_WIDTH), BF16),
        scratch_shapes=[pltpu.VMEM((s_len + 2 * CONV_HALO, CONV_WIDTH), F32)],
        compiler_params=_cparams(1),
        name="conv",
    )(z, conv_w_l, conv_b_l, g_l, beta_l)


def _outproj_kernel(x_ref, o_ref, zc_ref, gate_ref, mod_ref, wa_ref, wc_ref, wo_ref, gffn_ref,
                    wr_ref, br_ref, tri_ref,
                    x1_ref, h2_ref, te_ref, gw_ref, rank_ref, cnt_ref, carry_ref):
    i = pl.program_id(0)

    @pl.when(i == 0)
    def _():
        carry_ref[...] = jnp.zeros_like(carry_ref)

    m = mod_ref[0]
    attn_d = jnp.dot(o_ref[...], wa_ref[...], preferred_element_type=F32)
    conv_d = jnp.dot(zc_ref[...], wc_ref[...], preferred_element_type=F32)
    y = (gate_ref[:, 0:D_MODEL].astype(F32) * attn_d
         + gate_ref[:, D_MODEL:2 * D_MODEL].astype(F32) * conv_d)
    mix = jnp.dot(y.astype(BF16), wo_ref[...], preferred_element_type=F32)
    x1 = x_ref[...] + m[2:3] * mix
    x1_ref[...] = x1
    h2 = _rms_mod(x1, gffn_ref[...], m[3:4], m[4:5])
    h2_ref[...] = h2
    logits = jnp.dot(h2, wr_ref[...], preferred_element_type=F32,
                     precision=lax.Precision.HIGHEST) + br_ref[...]

    lane = lax.broadcasted_iota(jnp.int32, (TM, LANES), 1)
    lane_f = lane.astype(F32)
    work = logits
    sel_masks = []
    top_vals = []
    top_idx = []
    for _ in range(TOP_K):
        mx = jnp.max(work, axis=-1, keepdims=True)
        idx = jnp.min(jnp.where(work == mx, lane_f, float(LANES)), axis=-1, keepdims=True)
        sel = lane_f == idx
        sel_masks.append(sel)
        top_vals.append(mx)
        top_idx.append(idx)
        work = jnp.where(sel, NEG_BIG * 2.0, work)
    exps = [jnp.exp(v - top_vals[0]) for v in top_vals]
    denom = exps[0] + exps[1] + exps[2] + exps[3]

    onehot = jnp.zeros((TM, LANES), F32)
    for sel in sel_masks:
        onehot = onehot + jnp.where(sel, 1.0, 0.0)
    base = jnp.dot(tri_ref[...], onehot.astype(BF16), preferred_element_type=F32) + carry_ref[...]
    te = jnp.zeros((TM, LANES), F32)
    gw = jnp.zeros((TM, LANES), F32)
    rk = jnp.zeros((TM, LANES), F32)
    for kk in range(TOP_K):
        r = jnp.sum(jnp.where(sel_masks[kk], base, 0.0), axis=-1, keepdims=True)
        col = lane == kk
        te = jnp.where(col, top_idx[kk], te)
        gw = jnp.where(col, exps[kk] / denom, gw)
        rk = jnp.where(col, r, rk)
    te_ref[...] = te.astype(jnp.int32)
    gw_ref[...] = gw
    rank_ref[...] = rk.astype(jnp.int32)
    total = carry_ref[...] + jnp.sum(onehot, axis=0, keepdims=True)
    carry_ref[...] = total
    cnt_ref[...] = jnp.broadcast_to(total, cnt_ref.shape).astype(jnp.int32)


def _outproj(x, o, zc, gates, mod_l, wa_bf, wc_bf, wo_bf, l, gffn, wr_pad, br_pad, tri,
             n_ctx_tiles, tiles_per_seq):
    n_tok = x.shape[0]
    row = lambda i: (i, 0)
    const2 = lambda i: (0, 0)
    lay3 = lambda i: (l, 0, 0)
    modrow = lambda i: (_mod_row_of_tile(i, n_ctx_tiles, tiles_per_seq), 0, 0)
    return pl.pallas_call(
        _outproj_kernel,
        grid=(n_tok // TM,),
        in_specs=[
            pl.BlockSpec((TM, D_MODEL), row),
            pl.BlockSpec((TM, ATTN_WIDTH), row),
            pl.BlockSpec((TM, CONV_WIDTH), row),
            pl.BlockSpec((TM, 2 * D_MODEL), row),
            pl.BlockSpec((1, 6, D_MODEL), modrow),
            pl.BlockSpec((None, ATTN_WIDTH, D_MODEL), lay3),
            pl.BlockSpec((None, CONV_WIDTH, D_MODEL), lay3),
            pl.BlockSpec((None, D_MODEL, D_MODEL), lay3),
            pl.BlockSpec((1, D_MODEL), const2),
            pl.BlockSpec((D_MODEL, LANES), const2),
            pl.BlockSpec((1, LANES), const2),
            pl.BlockSpec((TM, TM), const2),
        ],
        out_specs=[
            pl.BlockSpec((TM, D_MODEL), row),
            pl.BlockSpec((TM, D_MODEL), row),
            pl.BlockSpec((TM, LANES), row),
            pl.BlockSpec((TM, LANES), row),
            pl.BlockSpec((TM, LANES), row),
            pl.BlockSpec((8, LANES), const2),
        ],
        out_shape=[
            jax.ShapeDtypeStruct((n_tok, D_MODEL), F32),
            jax.ShapeDtypeStruct((n_tok, D_MODEL), F32),
            jax.ShapeDtypeStruct((n_tok, LANES), jnp.int32),
            jax.ShapeDtypeStruct((n_tok, LANES), F32),
            jax.ShapeDtypeStruct((n_tok, LANES), jnp.int32),
            jax.ShapeDtypeStruct((8, LANES), jnp.int32),
        ],
        scratch_shapes=[pltpu.VMEM((1, LANES), F32)],
        compiler_params=_cparams(1),
        name="outproj_router",
    )(x, o, zc, gates, mod_l, wa_bf, wc_bf, wo_bf, gffn, wr_pad, br_pad, tri)


def _row_copy(src, src_row, dst, dst_row, sem):
    return pltpu.make_async_copy(src.at[pl.ds(src_row, 1), :], dst.at[pl.ds(dst_row, 1), :], sem)


def _dispatch_kernel(dest_ref, h_ref, xs_in_ref, xs_ref, sem):
    del xs_in_ref

    def issue(r, carry):
        for kk in range(TOP_K):
            _row_copy(h_ref, r, xs_ref, dest_ref[r * TOP_K + kk], sem).start()
        return carry

    lax.fori_loop(0, TM, issue, 0)

    def drain(r, carry):
        for kk in range(TOP_K):
            _row_copy(h_ref, r, xs_ref, dest_ref[r * TOP_K + kk], sem).wait()
        return carry

    lax.fori_loop(0, TM, drain, 0)


def _dispatch(h2, dest_flat, xs_zero):
    n_tok = h2.shape[0]
    return pl.pallas_call(
        _dispatch_kernel,
        grid=(n_tok // TM,),
        in_specs=[
            pl.BlockSpec((TM * TOP_K,), lambda i: (i,), memory_space=pltpu.SMEM),
            pl.BlockSpec((TM, D_MODEL), lambda i: (i, 0)),
            pl.BlockSpec(memory_space=pl.ANY),
        ],
        out_specs=pl.BlockSpec(memory_space=pl.ANY),
        out_shape=jax.ShapeDtypeStruct(xs_zero.shape, xs_zero.dtype),
        scratch_shapes=[pltpu.SemaphoreType.DMA(())],
        input_output_aliases={2: 0},
        compiler_params=_cparams(1),
        name="moe_dispatch",
    )(dest_flat, h2, xs_zero)


def _expert_kernel(be_ref, nused_ref, xs_ref, wgu_ref, bgu_ref, wd_ref, bd_ref, y_ref, wgu_bf, wd_bf):
    j = pl.program_id(0)

    @pl.when(j >= nused_ref[0])
    def _():
        y_ref[...] = jnp.zeros_like(y_ref)

    @pl.when(j < nused_ref[0])
    def _():
        prev = be_ref[jnp.maximum(j - 1, 0)]
        new_expert = jnp.logical_or(j == 0, be_ref[j] != prev)

        @pl.when(new_expert)
        def _():
            wgu_bf[...] = wgu_ref[...].astype(BF16)
            wd_bf[...] = wd_ref[...].astype(BF16)

        x = xs_ref[...].astype(BF16)
        gu = jnp.dot(x, wgu_bf[...], preferred_element_type=F32) + bgu_ref[...]
        g = jnp.minimum(gu[:, 0:D_EXPERT], SWIGLU_LIMIT)
        u = jnp.clip(gu[:, D_EXPERT:2 * D_EXPERT], -SWIGLU_LIMIT, SWIGLU_LIMIT)
        act = (u + 1.0) * (g * jax.nn.sigmoid(SWIGLU_ALPHA * g))
        y_ref[...] = jnp.dot(act.astype(BF16), wd_bf[...], preferred_element_type=F32) + bd_ref[...]


def _experts(xs, block_e, n_used, w_gate_up, b_gate_up, w_down, b_down, l):
    n_rows = xs.shape[0]
    n_blk = n_rows // EBLK
    blk = lambda j, be, nu: (jnp.minimum(j, nu[0] - 1), 0)
    grid_spec = pltpu.PrefetchScalarGridSpec(
        num_scalar_prefetch=2,
        grid=(n_blk,),
        in_specs=[
            pl.BlockSpec((EBLK, D_MODEL), blk),
            pl.BlockSpec((None, None, D_MODEL, 2 * D_EXPERT), lambda j, be, nu: (l, be[j], 0, 0)),
            pl.BlockSpec((None, None, 1, 2 * D_EXPERT), lambda j, be, nu: (l, be[j], 0, 0)),
            pl.BlockSpec((None, None, D_EXPERT, D_MODEL), lambda j, be, nu: (l, be[j], 0, 0)),
            pl.BlockSpec((None, None, 1, D_MODEL), lambda j, be, nu: (l, be[j], 0, 0)),
        ],
        out_specs=pl.BlockSpec((EBLK, D_MODEL), lambda j, be, nu: (j, 0)),
        scratch_shapes=[pltpu.VMEM((D_MODEL, 2 * D_EXPERT), BF16), pltpu.VMEM((D_EXPERT, D_MODEL), BF16)],
    )
    return pl.pallas_call(
        _expert_kernel,
        grid_spec=grid_spec,
        out_shape=jax.ShapeDtypeStruct((n_rows, D_MODEL), F32),
        compiler_params=_cparams(1),
        name="moe_experts",
    )(block_e, n_used, xs, w_gate_up,
      b_gate_up.reshape(DEPTH, N_EXPERTS, 1, 2 * D_EXPERT), w_down,
      b_down.reshape(DEPTH, N_EXPERTS, 1, D_MODEL))


def _combine_kernel(dest_ref, x1_ref, mod_ref, gw_ref, yb_ref, o_ref, buf_ref, sem):
    def issue(r, carry):
        for kk in range(TOP_K):
            _row_copy(yb_ref, dest_ref[r * TOP_K + kk], buf_ref.at[kk], r, sem).start()
        return carry

    lax.fori_loop(0, TM, issue, 0)

    def drain(r, carry):
        for kk in range(TOP_K):
            _row_copy(yb_ref, dest_ref[r * TOP_K + kk], buf_ref.at[kk], r, sem).wait()
        return carry

    lax.fori_loop(0, TM, drain, 0)

    gw = gw_ref[...]
    y = gw[:, 0:1] * buf_ref[0]
    for kk in range(1, TOP_K):
        y = y + gw[:, kk:kk + 1] * buf_ref[kk]
    o_ref[...] = x1_ref[...] + mod_ref[0][5:6] * y


def _combine(x1, mod_l, gatew, dest_flat, yb, n_ctx_tiles, tiles_per_seq):
    n_tok = x1.shape[0]
    row = lambda i: (i, 0)
    modrow = lambda i: (_mod_row_of_tile(i, n_ctx_tiles, tiles_per_seq), 0, 0)
    return pl.pallas_call(
        _combine_kernel,
        grid=(n_tok // TM,),
        in_specs=[
            pl.BlockSpec((TM * TOP_K,), lambda i: (i,), memory_space=pltpu.SMEM),
            pl.BlockSpec((TM, D_MODEL), row),
            pl.BlockSpec((1, 6, D_MODEL), modrow),
            pl.BlockSpec((TM, LANES), row),
            pl.BlockSpec(memory_space=pl.ANY),
        ],
        out_specs=pl.BlockSpec((TM, D_MODEL), row),
        out_shape=jax.ShapeDtypeStruct((n_tok, D_MODEL), F32),
        scratch_shapes=[pltpu.VMEM((TOP_K, TM, D_MODEL), F32), pltpu.SemaphoreType.DMA(())],
        compiler_params=_cparams(1),
        name="moe_combine",
    )(dest_flat, x1, mod_l, gatew, yb)


def _rope_tables(s_len):
    rows = s_len // GRID_W
    row = jnp.repeat(jnp.arange(rows, dtype=F32), GRID_W)
    col = jnp.tile(jnp.arange(GRID_W, dtype=F32), rows)
    n_freq = HEAD_DIM // 4
    inv = ROPE_BASE ** (-jnp.arange(n_freq, dtype=F32) / n_freq)
    ang_r = row[:, None] * inv
    ang_c = col[:, None] * inv
    ang = jnp.concatenate([ang_r, ang_r, ang_c, ang_c], axis=1)
    sign = jnp.tile(jnp.concatenate([-jnp.ones((n_freq,), F32), jnp.ones((n_freq,), F32)]), 2)
    cos = jnp.tile(jnp.cos(ang), (1, 2))
    sin = jnp.tile(jnp.sin(ang) * sign, (1, 2))
    cos = jnp.concatenate([jnp.ones((TM, LANES), F32), cos], axis=0)
    sin = jnp.concatenate([jnp.zeros((TM, LANES), F32), sin], axis=0)
    return cos, sin


def kernel(x_prompt, x_sample, c, cache_k, cache_v, c_ctx, w_ada, b_ada, norm_mix, norm_ffn, w_in, q_norm, k_norm, lam_qk, sub_norm, w_attn_out, conv_w, conv_b, conv_norm_g, conv_norm_b, w_conv_out, w_out, w_router, b_router, w_gate_up, b_gate_up, w_down, b_down):
    b_p, s_p, _ = x_prompt.shape
    b_s, s_s, _ = x_sample.shape
    past = cache_k.shape[2]
    n_ctx = b_p * s_p
    n_tok = n_ctx + b_s * s_s
    n_ctx_tiles = n_ctx // TM
    tiles_per_seq = s_s // TM
    assert s_p == TM and s_s % TM == 0 and n_ctx % s_s == 0 and s_s % GRID_W == 0

    x = jnp.concatenate([x_prompt.reshape(n_ctx, D_MODEL), x_sample.reshape(b_s * s_s, D_MODEL)], axis=0)
    n_cond = 8 * pl.cdiv(1 + b_s, 8)
    cond = jnp.zeros((n_cond, D_MODEL), F32).at[0].set(c_ctx).at[1:1 + b_s].set(c)
    mod = _ada_all_layers(cond, w_ada.astype(BF16), b_ada).reshape(DEPTH, n_cond, 6, D_MODEL)

    w_in_bf = w_in.astype(BF16)
    wa_bf = w_attn_out.astype(BF16)
    wc_bf = w_conv_out.astype(BF16)
    wo_bf = w_out.astype(BF16)
    ck = cache_k.reshape(b_s, DEPTH, past, ATTN_WIDTH)
    cv = cache_v.reshape(b_s, DEPTH, past, ATTN_WIDTH)
    cos_t, sin_t = _rope_tables(s_s)
    grp = jnp.arange(ATTN_WIDTH, dtype=jnp.int32) // HEAD_DIM
    gsum = (grp[:, None] == grp[None, :]).astype(BF16)
    tri = (jnp.arange(TM)[:, None] > jnp.arange(TM)[None, :]).astype(BF16)
    n_blk = (n_tok * TOP_K) // EBLK + N_EXPERTS
    n_rows = n_blk * EBLK

    new_k, new_v = [], []
    for l in range(DEPTH):
        lam_init = 0.8 - 0.6 * math.exp(-0.3 * l)
        q, k, v, zpre, gates = _inproj(
            x, mod[l], norm_mix[l].reshape(1, D_MODEL), w_in_bf, l,
            jnp.tile(q_norm[l], ATTN_WIDTH // HEAD_DIM).reshape(1, ATTN_WIDTH),
            jnp.tile(k_norm[l], ATTN_WIDTH // HEAD_DIM).reshape(1, ATTN_WIDTH),
            gsum, cos_t, sin_t, n_ctx_tiles, tiles_per_seq)
        new_k.append(k[:n_ctx].reshape(b_p, s_p, N_HEADS, 2 * HEAD_DIM))
        new_v.append(v[:n_ctx].reshape(b_p, s_p, N_HEADS, 2 * HEAD_DIM))
        sub_l = sub_norm[l].reshape(1, LANES)
        o_p = _attention(q, k, v, lam_qk[l], sub_l, lam_init, b_p, s_p, 0)
        o_s = _attention(q, k, v, lam_qk[l], sub_l, lam_init, b_s, s_s, n_ctx // s_s, ctx=(ck, cv, l))
        conv_args = (conv_w[l], conv_b[l].reshape(1, CONV_WIDTH), conv_norm_g[l].reshape(1, CONV_WIDTH),
                     conv_norm_b[l].reshape(1, CONV_WIDTH))
        zc_p = _conv_module(zpre, *conv_args, b_p, s_p, 0)
        zc_s = _conv_module(zpre, *conv_args, b_s, s_s, n_ctx // s_s)
        o = jnp.concatenate([o_p, o_s], axis=0)
        zc = jnp.concatenate([zc_p, zc_s], axis=0)
        wr_pad = jnp.zeros((D_MODEL, LANES), F32).at[:, :N_EXPERTS].set(w_router[l])
        br_pad = jnp.full((1, LANES), NEG_BIG, F32).at[0, :N_EXPERTS].set(b_router[l])
        x1, h2, top_e, gatew, rank, counts = _outproj(
            x, o, zc, gates, mod[l], wa_bf, wc_bf, wo_bf, l, norm_ffn[l].reshape(1, D_MODEL),
            wr_pad, br_pad, tri, n_ctx_tiles, tiles_per_seq)

        cnt = counts[0, :N_EXPERTS]
        padded = (cnt + EBLK - 1) // EBLK * EBLK
        pend = jnp.cumsum(padded)
        pstart = pend - padded
        dest = (pstart[top_e[:, :TOP_K]] + rank[:, :TOP_K]).reshape(n_tok * TOP_K)
        block_e = jnp.minimum(
            jnp.searchsorted(pend, jnp.arange(n_blk, dtype=jnp.int32) * EBLK, side='right'),
            N_EXPERTS - 1).astype(jnp.int32)
        n_used = (pend[-1] // EBLK).astype(jnp.int32).reshape(1)

        xs = _dispatch(h2, dest, jnp.zeros((n_rows, D_MODEL), F32))
        yb = _experts(xs, block_e, n_used, w_gate_up, b_gate_up, w_down, b_down, l)
        x = _combine(x1, mod[l], gatew, dest, yb, n_ctx_tiles, tiles_per_seq)

    y_prompt = x[:n_ctx].reshape(b_p, s_p, D_MODEL)
    y_sample = x[n_ctx:].reshape(b_s, s_s, D_MODEL)
    return (y_prompt, y_sample, jnp.stack(new_k, axis=1), jnp.stack(new_v, axis=1))
```

```python
import functools
import math

import jax
import jax.numpy as jnp
from jax import lax
from jax.experimental import pallas as pl
from jax.experimental.pallas import tpu as pltpu

F32 = jnp.float32
BF16 = jnp.bfloat16

D_MODEL = 1024
DEPTH = 4
GRID_W = 64
ATTN_WIDTH = 512
N_HEADS = 4
HEAD_DIM = 64
CONV_WIDTH = 512
CONV_KERNEL = 31
N_EXPERTS = 32
TOP_K = 4
D_EXPERT = 1024
SWIGLU_LIMIT = 7.0
SWIGLU_ALPHA = 1.702
ROPE_BASE = 10000.0
EPS = 1e-6

LANES = 128
TM = 256
TMO = 512
EBLK = 256
INV_CHUNK = 4096
ADA_TN = 1024
CONV_CHUNK = 64
CONV_HALO = 16
NEG_BIG = -1e30
VMEM_LIMIT = 56 * 1024 * 1024


def _cparams(n_axes, vmem=None):
    return pltpu.CompilerParams(
        dimension_semantics=("arbitrary",) * n_axes,
        vmem_limit_bytes=VMEM_LIMIT if vmem is None else vmem)


def _mod_row_of_tile(i, n_ctx_tiles, tiles_per_seq):
    return jnp.where(i < n_ctx_tiles, 0, 1 + (i - n_ctx_tiles) // tiles_per_seq)


def _ada_kernel(cond_ref, w_ref, b_ref, o_ref):
    cnd = cond_ref[...]
    act = cnd * jax.nn.sigmoid(cnd)
    o_ref[0] = jnp.dot(act.astype(BF16), w_ref[0], preferred_element_type=F32) + b_ref[0]


def _ada_all_layers(cond, w_ada_bf, b_ada):
    n_rows = cond.shape[0]
    n_col = w_ada_bf.shape[2]
    return pl.pallas_call(
        _ada_kernel,
        grid=(DEPTH, n_col // ADA_TN),
        in_specs=[
            pl.BlockSpec((n_rows, D_MODEL), lambda l, j: (0, 0)),
            pl.BlockSpec((1, D_MODEL, ADA_TN), lambda l, j: (l, 0, j)),
            pl.BlockSpec((1, 1, ADA_TN), lambda l, j: (l, 0, j)),
        ],
        out_specs=pl.BlockSpec((1, n_rows, ADA_TN), lambda l, j: (l, 0, j)),
        out_shape=jax.ShapeDtypeStruct((DEPTH, n_rows, n_col), F32),
        compiler_params=_cparams(2),
        name="ada",
    )(cond, w_ada_bf, b_ada.reshape(DEPTH, 1, n_col))


def _rms_mod(x, g, shift, scale):
    y = x * lax.rsqrt(jnp.mean(x * x, axis=-1, keepdims=True) + EPS) * g
    return y * (1.0 + scale) + shift


def _group_mean_sq(t, gsum):
    s = t * t
    hi = s.astype(BF16)
    lo = (s - hi.astype(F32)).astype(BF16)
    tot = jnp.dot(hi, gsum, preferred_element_type=F32) + jnp.dot(lo, gsum, preferred_element_type=F32)
    return tot * (1.0 / HEAD_DIM)


def _rope_slab(x, cos, sgn_sin, first_half):
    partner = jnp.where(first_half, pltpu.roll(x, LANES - 16, 1), pltpu.roll(x, 16, 1))
    return x * cos + partner * sgn_sin


def _inproj_kernel(x_ref, mod_ref, gmix_ref, w_ref, qn_ref, kn_ref, gsum_ref, cos_ref, sin_ref,
                   q_ref, k_ref, v_ref, z_ref, gate_ref):
    m = mod_ref[0]
    h = _rms_mod(x_ref[...], gmix_ref[...], m[0:1], m[1:2]).astype(BF16)
    aw = ATTN_WIDTH
    q = jnp.dot(h, w_ref[:, 0:aw], preferred_element_type=F32)
    k = jnp.dot(h, w_ref[:, aw:2 * aw], preferred_element_type=F32)
    gsum = gsum_ref[...]
    q = q * lax.rsqrt(_group_mean_sq(q, gsum) + EPS) * qn_ref[...]
    k = k * lax.rsqrt(_group_mean_sq(k, gsum) + EPS) * kn_ref[...]
    cos = cos_ref[...]
    sin = sin_ref[...]
    lane = lax.broadcasted_iota(jnp.int32, (TM, LANES), 1)
    first_half = (lane & 16) == 0
    for hd in range(N_HEADS):
        sl = slice(hd * LANES, (hd + 1) * LANES)
        q_ref[:, sl] = (_rope_slab(q[:, sl], cos, sin, first_half) * (HEAD_DIM ** -0.5)).astype(BF16)
        k_ref[:, sl] = _rope_slab(k[:, sl], cos, sin, first_half)
    v_ref[...] = jnp.dot(h, w_ref[:, 2 * aw:3 * aw], preferred_element_type=F32)
    c0 = 3 * aw
    a = jnp.dot(h, w_ref[:, c0:c0 + CONV_WIDTH], preferred_element_type=F32)
    g = jnp.dot(h, w_ref[:, c0 + CONV_WIDTH:c0 + 2 * CONV_WIDTH], preferred_element_type=F32)
    z_ref[...] = a * jax.nn.sigmoid(g)
    c1 = c0 + 2 * CONV_WIDTH
    gl = jnp.dot(h, w_ref[:, c1:c1 + 2 * D_MODEL], preferred_element_type=F32)
    gate_ref[...] = jax.nn.sigmoid(gl).astype(BF16)


def _inproj(x, mod_l, gmix, w_in_bf, l, qn, kn, gsum, cos_t, sin_t, n_ctx_tiles, tiles_per_seq):
    n_tok = x.shape[0]
    in_w = w_in_bf.shape[2]
    row = lambda i: (i, 0)
    modrow = lambda i: (_mod_row_of_tile(i, n_ctx_tiles, tiles_per_seq), 0, 0)
    ropeblk = lambda i: (jnp.where(i < n_ctx_tiles, 0, 1 + (i - n_ctx_tiles) % tiles_per_seq), 0)
    const2 = lambda i: (0, 0)
    return pl.pallas_call(
        _inproj_kernel,
        grid=(n_tok // TM,),
        in_specs=[
            pl.BlockSpec((TM, D_MODEL), row),
            pl.BlockSpec((1, 6, D_MODEL), modrow),
            pl.BlockSpec((1, D_MODEL), const2),
            pl.BlockSpec((None, D_MODEL, in_w), lambda i: (l, 0, 0)),
            pl.BlockSpec((1, ATTN_WIDTH), const2),
            pl.BlockSpec((1, ATTN_WIDTH), const2),
            pl.BlockSpec((ATTN_WIDTH, ATTN_WIDTH), const2),
            pl.BlockSpec((TM, LANES), ropeblk),
            pl.BlockSpec((TM, LANES), ropeblk),
        ],
        out_specs=[
            pl.BlockSpec((TM, ATTN_WIDTH), row),
            pl.BlockSpec((TM, ATTN_WIDTH), row),
            pl.BlockSpec((TM, ATTN_WIDTH), row),
            pl.BlockSpec((TM, CONV_WIDTH), row),
            pl.BlockSpec((TM, 2 * D_MODEL), row),
        ],
        out_shape=[
            jax.ShapeDtypeStruct((n_tok, ATTN_WIDTH), BF16),
            jax.ShapeDtypeStruct((n_tok, ATTN_WIDTH), F32),
            jax.ShapeDtypeStruct((n_tok, ATTN_WIDTH), F32),
            jax.ShapeDtypeStruct((n_tok, CONV_WIDTH), F32),
            jax.ShapeDtypeStruct((n_tok, 2 * D_MODEL), BF16),
        ],
        compiler_params=_cparams(1),
        name="inproj",
    )(x, mod_l, gmix, w_in_bf, qn, kn, gsum, cos_t, sin_t)


def _attn_kernel(*refs, lam_init, s_q, s_ctx, q_chunk):
    if s_ctx:
        q_ref, k_ref, v_ref, ck_ref, cv_ref, lam_ref, sub_ref, o_ref, kc_ref, vc_ref = refs
        kc_ref[0:s_ctx, :] = ck_ref[...].astype(BF16)
        vc_ref[0:s_ctx, :] = cv_ref[...].astype(BF16)
    else:
        q_ref, k_ref, v_ref, lam_ref, sub_ref, o_ref, kc_ref, vc_ref = refs
    kc_ref[s_ctx:s_ctx + s_q, :] = k_ref[...].astype(BF16)
    vc_ref[s_ctx:s_ctx + s_q, :] = v_ref[...].astype(BF16)
    lq = lam_ref[...]
    lam = (jnp.exp(jnp.sum(lq[0:1] * lq[1:2], axis=-1, keepdims=True))
           - jnp.exp(jnp.sum(lq[2:3] * lq[3:4], axis=-1, keepdims=True)) + lam_init)
    lane = lax.broadcasted_iota(jnp.int32, (q_chunk, LANES), 1)
    map0 = lane < HEAD_DIM
    sub = sub_ref[...]
    nt = (((1,), (1,)), ((), ()))

    def chunk(c, carry):
        r0 = pl.multiple_of(c * q_chunk, q_chunk)
        q = q_ref[pl.ds(r0, q_chunk), :]
        kc = kc_ref[...]
        zero = jnp.zeros_like(q)
        probs = []
        for mp in range(2):
            qm = jnp.where(map0, q, zero) if mp == 0 else jnp.where(map0, zero, q)
            s = lax.dot_general(qm, kc, nt, preferred_element_type=F32)
            e = jnp.exp(s - jnp.max(s, axis=-1, keepdims=True))
            probs.append(e / jnp.sum(e, axis=-1, keepdims=True))
        w = probs[0] - lam * probs[1]
        o = jnp.dot(w.astype(BF16), vc_ref[...], preferred_element_type=F32)
        o = o * lax.rsqrt(jnp.mean(o * o, axis=-1, keepdims=True) + EPS) * sub * (1.0 - lam_init)
        o_ref[pl.ds(r0, q_chunk), :] = o.astype(BF16)
        return carry

    lax.fori_loop(0, s_q // q_chunk, chunk, 0)


def _attention(q, k, v, lam_qk_l, sub_l, lam_init, n_batch, s_q, row_blk0, ctx=None):
    s_ctx = 0 if ctx is None else ctx[0].shape[2]
    own = pl.BlockSpec((s_q, LANES), lambda b, h: (row_blk0 + b, h))
    in_specs = [own, own, own]
    args = [q, k, v]
    if ctx is not None:
        ck, cv, l = ctx
        cspec = pl.BlockSpec((None, None, s_ctx, LANES), lambda b, h: (b, l, 0, h))
        in_specs += [cspec, cspec]
        args += [ck, cv]
    in_specs += [pl.BlockSpec((4, HEAD_DIM), lambda b, h: (0, 0)),
                 pl.BlockSpec((1, LANES), lambda b, h: (0, 0))]
    args += [lam_qk_l, sub_l]
    return pl.pallas_call(
        functools.partial(_attn_kernel, lam_init=lam_init, s_q=s_q, s_ctx=s_ctx, q_chunk=min(s_q, 256)),
        grid=(n_batch, N_HEADS),
        in_specs=in_specs,
        out_specs=pl.BlockSpec((s_q, LANES), lambda b, h: (b, h)),
        out_shape=jax.ShapeDtypeStruct((n_batch * s_q, ATTN_WIDTH), BF16),
        scratch_shapes=[pltpu.VMEM((s_ctx + s_q, LANES), BF16), pltpu.VMEM((s_ctx + s_q, LANES), BF16)],
        compiler_params=_cparams(2),
        name="attn_ctx" if ctx is not None else "attn",
    )(*args)


def _conv_kernel(z_ref, w_ref, b_ref, g_ref, beta_ref, o_ref, pad_ref, *, s_len):
    halo = jnp.zeros((CONV_HALO, CONV_WIDTH), F32)
    pad_ref[0:CONV_HALO, :] = halo
    pad_ref[CONV_HALO:CONV_HALO + s_len, :] = z_ref[...]
    pad_ref[CONV_HALO + s_len:2 * CONV_HALO + s_len, :] = halo
    bias = b_ref[...]
    gam = g_ref[...]
    beta = beta_ref[...]
    off0 = CONV_HALO - CONV_KERNEL // 2

    def chunk(c, carry):
        r0 = pl.multiple_of(c * CONV_CHUNK, CONV_CHUNK)
        slabs = []
        for s0 in range(0, CONV_WIDTH, LANES):
            sl = slice(s0, s0 + LANES)
            win = pad_ref[pl.ds(r0, CONV_CHUNK + 2 * CONV_HALO), sl]
            part = jnp.broadcast_to(bias[:, sl], (CONV_CHUNK, LANES))
            for res in range(8):
                shifted = win[res:res + CONV_CHUNK + 2 * CONV_HALO - 8]
                for a8 in range(0, 2 * CONV_HALO, 8):
                    j = a8 + res - off0
                    if 0 <= j < CONV_KERNEL:
                        part = part + shifted[a8:a8 + CONV_CHUNK] * w_ref[j:j + 1, sl]
            slabs.append(part)
        acc = jnp.concatenate(slabs, axis=-1)
        mu = jnp.mean(acc, axis=-1, keepdims=True)
        d = acc - mu
        var = jnp.mean(d * d, axis=-1, keepdims=True)
        y = d * lax.rsqrt(var + EPS) * gam + beta
        o_ref[pl.ds(r0, CONV_CHUNK), :] = (y * jax.nn.sigmoid(y)).astype(BF16)
        return carry

    lax.fori_loop(0, s_len // CONV_CHUNK, chunk, 0)


def _conv_module(z, conv_w_l, conv_b_l, g_l, beta_l, n_batch, s_len, row_blk0):
    const2 = lambda b: (0, 0)
    return pl.pallas_call(
        functools.partial(_conv_kernel, s_len=s_len),
        grid=(n_batch,),
        in_specs=[
            pl.BlockSpec((s_len, CONV_WIDTH), lambda b: (row_blk0 + b, 0)),
            pl.BlockSpec((CONV_KERNEL, CONV_WIDTH), const2),
            pl.BlockSpec((1, CONV_WIDTH), const2),
            pl.BlockSpec((1, CONV_WIDTH), const2),
            pl.BlockSpec((1, CONV_WIDTH), const2),
        ],
        out_specs=pl.BlockSpec((s_len, CONV_WIDTH), lambda b: (b, 0)),
        out_shape=jax.ShapeDtypeStruct((n_batch * s_len, CONV_WIDTH), BF16),
        scratch_shapes=[pltpu.VMEM((s_len + 2 * CONV_HALO, CONV_WIDTH), F32)],
        compiler_params=_cparams(1),
        name="conv",
    )(z, conv_w_l, conv_b_l, g_l, beta_l)


def _route_rows(logits, seen, tri):
    rows = logits.shape[0]
    lane = lax.broadcasted_iota(jnp.int32, (rows, LANES), 1)
    lane_f = lane.astype(F32)
    work = logits
    sel_masks, top_vals, top_idx = [], [], []
    for _ in range(TOP_K):
        mx = jnp.max(work, axis=-1, keepdims=True)
        idx = jnp.min(jnp.where(work == mx, lane_f, float(LANES)), axis=-1, keepdims=True)
        sel = lane_f == idx
        sel_masks.append(sel)
        top_vals.append(mx)
        top_idx.append(idx)
        work = jnp.where(sel, NEG_BIG * 2.0, work)
    exps = [jnp.exp(v - top_vals[0]) for v in top_vals]
    denom = exps[0] + exps[1] + exps[2] + exps[3]
    onehot = jnp.zeros((rows, LANES), F32)
    for sel in sel_masks:
        onehot = onehot + jnp.where(sel, 1.0, 0.0)
    base = jnp.dot(tri, onehot.astype(BF16), preferred_element_type=F32) + seen
    te = jnp.zeros((rows, LANES), F32)
    gw = jnp.zeros((rows, LANES), F32)
    rk = jnp.zeros((rows, LANES), F32)
    for kk in range(TOP_K):
        r = jnp.sum(jnp.where(sel_masks[kk], base, 0.0), axis=-1, keepdims=True)
        col = lane == kk
        te = jnp.where(col, top_idx[kk], te)
        gw = jnp.where(col, exps[kk] / denom, gw)
        rk = jnp.where(col, r, rk)
    return te.astype(jnp.int32), gw, rk.astype(jnp.int32), jnp.sum(onehot, axis=0, keepdims=True)


def _outproj_kernel(x_ref, op_ref, os_ref, zp_ref, zs_ref, gate_ref, mod_ref, wa_ref, wc_ref, wo_ref,
                    gffn_ref, wrh_ref, wrl_ref, br_ref, tri_ref,
                    x1_ref, h2_ref, te_ref, gw_ref, rank_ref, cnt_ref, carry_ref, *, n_ctx_tiles):
    i = pl.program_id(0)

    @pl.when(i == 0)
    def _():
        carry_ref[...] = jnp.zeros_like(carry_ref)

    m = mod_ref[0]
    is_ctx = i < n_ctx_tiles
    seen = carry_ref[...]
    for g0 in range(0, TMO, TM):
        rs = slice(g0, g0 + TM)
        o = jnp.where(is_ctx, op_ref[rs, :], os_ref[rs, :])
        zc = jnp.where(is_ctx, zp_ref[rs, :], zs_ref[rs, :])
        attn_d = jnp.dot(o, wa_ref[...], preferred_element_type=F32)
        conv_d = jnp.dot(zc, wc_ref[...], preferred_element_type=F32)
        y = (gate_ref[rs, 0:D_MODEL].astype(F32) * attn_d
             + gate_ref[rs, D_MODEL:2 * D_MODEL].astype(F32) * conv_d)
        mix = jnp.dot(y.astype(BF16), wo_ref[...], preferred_element_type=F32)
        x1 = x_ref[rs, :] + m[2:3] * mix
        x1_ref[rs, :] = x1
        h2 = _rms_mod(x1, gffn_ref[...], m[3:4], m[4:5])
        h2_ref[rs, :] = h2
        h_hi = h2.astype(BF16)
        h_lo = (h2 - h_hi.astype(F32)).astype(BF16)
        logits = (jnp.dot(h_hi, wrh_ref[...], preferred_element_type=F32)
                  + (jnp.dot(h_lo, wrh_ref[...], preferred_element_type=F32)
                     + jnp.dot(h_hi, wrl_ref[...], preferred_element_type=F32))) + br_ref[...]
        te, gw, rk, cnt = _route_rows(logits, seen, tri_ref[...])
        te_ref[rs, :] = te
        gw_ref[rs, :] = gw
        rank_ref[rs, :] = rk
        seen = seen + cnt
    carry_ref[...] = seen
    cnt_ref[...] = jnp.broadcast_to(seen, cnt_ref.shape).astype(jnp.int32)


def _outproj(x, o_p, o_s, zc_p, zc_s, gates, mod_l, wa_bf, wc_bf, wo_bf, l, gffn, wr_hi, wr_lo, br_pad,
             tri, s_lat):
    n_tok = x.shape[0]
    n_ctx_tiles = o_p.shape[0] // TMO
    n_lat_tiles = o_s.shape[0] // TMO
    row = lambda i: (i, 0)
    ctxrow = lambda i: (jnp.minimum(i, n_ctx_tiles - 1), 0)
    latrow = lambda i: (jnp.clip(i - n_ctx_tiles, 0, n_lat_tiles - 1), 0)
    const2 = lambda i: (0, 0)
    lay3 = lambda i: (l, 0, 0)
    modrow = lambda i: (_mod_row_of_tile(i, n_ctx_tiles, s_lat // TMO), 0, 0)
    return pl.pallas_call(
        functools.partial(_outproj_kernel, n_ctx_tiles=n_ctx_tiles),
        grid=(n_tok // TMO,),
        in_specs=[
            pl.BlockSpec((TMO, D_MODEL), row),
            pl.BlockSpec((TMO, ATTN_WIDTH), ctxrow),
            pl.BlockSpec((TMO, ATTN_WIDTH), latrow),
            pl.BlockSpec((TMO, CONV_WIDTH), ctxrow),
            pl.BlockSpec((TMO, CONV_WIDTH), latrow),
            pl.BlockSpec((TMO, 2 * D_MODEL), row),
            pl.BlockSpec((1, 6, D_MODEL), modrow),
            pl.BlockSpec((None, ATTN_WIDTH, D_MODEL), lay3),
            pl.BlockSpec((None, CONV_WIDTH, D_MODEL), lay3),
            pl.BlockSpec((None, D_MODEL, D_MODEL), lay3),
            pl.BlockSpec((1, D_MODEL), const2),
            pl.BlockSpec((D_MODEL, LANES), const2),
            pl.BlockSpec((D_MODEL, LANES), const2),
            pl.BlockSpec((1, LANES), const2),
            pl.BlockSpec((TM, TM), const2),
        ],
        out_specs=[
            pl.BlockSpec((TMO, D_MODEL), row),
            pl.BlockSpec((TMO, D_MODEL), row),
            pl.BlockSpec((TMO, LANES), row),
            pl.BlockSpec((TMO, LANES), row),
            pl.BlockSpec((TMO, LANES), row),
            pl.BlockSpec((8, LANES), const2),
        ],
        out_shape=[
            jax.ShapeDtypeStruct((n_tok, D_MODEL), F32),
            jax.ShapeDtypeStruct((n_tok, D_MODEL), F32),
            jax.ShapeDtypeStruct((n_tok, LANES), jnp.int32),
            jax.ShapeDtypeStruct((n_tok, LANES), F32),
            jax.ShapeDtypeStruct((n_tok, LANES), jnp.int32),
            jax.ShapeDtypeStruct((8, LANES), jnp.int32),
        ],
        scratch_shapes=[pltpu.VMEM((1, LANES), F32)],
        compiler_params=_cparams(1),
        name="outproj_router",
    )(x, o_p, o_s, zc_p, zc_s, gates, mod_l, wa_bf, wc_bf, wo_bf, gffn, wr_hi, wr_lo, br_pad, tri)


def _invert_kernel(te_ref, rk_ref, pstart_ref, init_ref, inv_ref, sem):
    i = pl.program_id(0)

    @pl.when(i == 0)
    def _():
        cp = pltpu.make_async_copy(init_ref, inv_ref, sem)
        cp.start()
        cp.wait()

    base = i * INV_CHUNK

    def body(a, carry):
        inv_ref[pstart_ref[te_ref[a]] + rk_ref[a]] = base + a
        return carry

    lax.fori_loop(0, INV_CHUNK, body, 0, unroll=8)


def _invert(te_flat, rk_flat, pstart, inv_init):
    n_assign = te_flat.shape[0]
    chunk = lambda i: (i,)
    return pl.pallas_call(
        _invert_kernel,
        grid=(n_assign // INV_CHUNK,),
        in_specs=[
            pl.BlockSpec((INV_CHUNK,), chunk, memory_space=pltpu.SMEM),
            pl.BlockSpec((INV_CHUNK,), chunk, memory_space=pltpu.SMEM),
            pl.BlockSpec(memory_space=pltpu.SMEM),
            pl.BlockSpec(memory_space=pl.ANY),
        ],
        out_specs=pl.BlockSpec(memory_space=pltpu.SMEM),
        out_shape=jax.ShapeDtypeStruct(inv_init.shape, jnp.int32),
        scratch_shapes=[pltpu.SemaphoreType.DMA(())],
        compiler_params=_cparams(1),
        name="moe_invert",
    )(te_flat, rk_flat, pstart, inv_init)


def _expert_kernel(be_ref, nused_ref, inv_out_ref, inv_in_ref, h_ref, wgu_ref, bgu_ref, wd_ref, bd_ref,
                   y4_ref, wgu_bf, wd_bf, xbuf0, xbuf1, ybuf0, ybuf1, gsem, ssem, *, n_tok, n_blk):
    j = pl.program_id(0)
    n_used = nused_ref[0]
    n_assign = n_tok * TOP_K
    xbuf = (xbuf0, xbuf1)
    ybuf = (ybuf0, ybuf1)

    def start_gather(slot):
        for r in range(EBLK):
            tok = jnp.minimum(inv_in_ref[r] >> 2, n_tok - 1)
            pltpu.make_async_copy(h_ref.at[pl.ds(tok, 1), :], xbuf[slot].at[pl.ds(r, 1), :],
                                  gsem.at[slot]).start()

    def wait_gather(slot):
        pltpu.make_async_copy(h_ref.at[pl.ds(0, EBLK), :], xbuf[slot], gsem.at[slot]).wait()

    def start_scatter(slot, real):
        for r in range(EBLK):
            row = jnp.where(real, inv_out_ref[r], n_assign + r)
            pltpu.make_async_copy(ybuf[slot].at[pl.ds(r, 1), :], y4_ref.at[pl.ds(row, 1), :],
                                  ssem.at[slot]).start()

    def wait_scatter(slot):
        pltpu.make_async_copy(ybuf[slot], y4_ref.at[pl.ds(0, EBLK), :], ssem.at[slot]).wait()

    @pl.when(j == 0)
    def _():
        ybuf0[...] = jnp.zeros_like(ybuf0)
        ybuf1[...] = jnp.zeros_like(ybuf1)
        start_gather(0)

    is_compute = jnp.logical_and(j >= 1, j <= n_used)
    blk = j - 1
    e_cur = be_ref[jnp.clip(blk, 0, n_blk - 1)]
    e_prev = be_ref[jnp.clip(blk - 1, 0, n_blk - 1)]

    @pl.when(jnp.logical_and(is_compute, jnp.logical_or(j == 1, e_cur != e_prev)))
    def _():
        wgu_bf[...] = wgu_ref[...].astype(BF16)
        wd_bf[...] = wd_ref[...].astype(BF16)

    for slot in range(2):
        other = 1 - slot

        @pl.when(jnp.logical_and(is_compute, blk % 2 == slot))
        def _():
            wait_gather(slot)

            @pl.when(j >= 2)
            def _():
                wait_scatter(slot)

            start_gather(other)
            start_scatter(other, j >= 2)
            x = xbuf[slot][...].astype(BF16)
            gu = jnp.dot(x, wgu_bf[...], preferred_element_type=F32) + bgu_ref[...]
            g = jnp.minimum(gu[:, 0:D_EXPERT], SWIGLU_LIMIT)
            u = jnp.clip(gu[:, D_EXPERT:2 * D_EXPERT], -SWIGLU_LIMIT, SWIGLU_LIMIT)
            act = (u + 1.0) * (g * jax.nn.sigmoid(SWIGLU_ALPHA * g))
            ybuf[slot][...] = (jnp.dot(act.astype(BF16), wd_bf[...], preferred_element_type=F32)
                               + bd_ref[...])

        @pl.when(jnp.logical_and(j == n_used + 1, (n_used - 1) % 2 == slot))
        def _():
            wait_gather(other)
            wait_scatter(other)
            start_scatter(slot, True)
            wait_scatter(slot)


def _experts(h2, inv, block_e, n_used, w_gate_up, b_gate_up, w_down, b_down, l):
    n_tok = h2.shape[0]
    n_rows = inv.shape[0]
    n_blk = n_rows // EBLK
    wblk = lambda j, be, nu: (l, be[jnp.clip(j - 1, 0, nu[0] - 1)], 0, 0)
    grid_spec = pltpu.PrefetchScalarGridSpec(
        num_scalar_prefetch=2,
        grid=(n_blk + 2,),
        in_specs=[
            pl.BlockSpec((EBLK,), lambda j, be, nu: (jnp.clip(j - 2, 0, n_blk - 1),),
                         memory_space=pltpu.SMEM),
            pl.BlockSpec((EBLK,), lambda j, be, nu: (jnp.minimum(j, n_blk - 1),),
                         memory_space=pltpu.SMEM),
            pl.BlockSpec(memory_space=pl.ANY),
            pl.BlockSpec((None, None, D_MODEL, 2 * D_EXPERT), wblk),
            pl.BlockSpec((None, None, 1, 2 * D_EXPERT), wblk),
            pl.BlockSpec((None, None, D_EXPERT, D_MODEL), wblk),
            pl.BlockSpec((None, None, 1, D_MODEL), wblk),
        ],
        out_specs=pl.BlockSpec(memory_space=pl.ANY),
        scratch_shapes=[
            pltpu.VMEM((D_MODEL, 2 * D_EXPERT), BF16),
            pltpu.VMEM((D_EXPERT, D_MODEL), BF16),
            pltpu.VMEM((EBLK, D_MODEL), F32),
            pltpu.VMEM((EBLK, D_MODEL), F32),
            pltpu.VMEM((EBLK, D_MODEL), F32),
            pltpu.VMEM((EBLK, D_MODEL), F32),
            pltpu.SemaphoreType.DMA((2,)),
            pltpu.SemaphoreType.DMA((2,)),
        ],
    )
    return pl.pallas_call(
        functools.partial(_expert_kernel, n_tok=n_tok, n_blk=n_blk),
        grid_spec=grid_spec,
        out_shape=jax.ShapeDtypeStruct((n_tok * TOP_K + n_rows, D_MODEL), F32),
        compiler_params=_cparams(1),
        name="moe_experts",
    )(block_e, n_used, inv, inv, h2, w_gate_up,
      b_gate_up.reshape(DEPTH, N_EXPERTS, 1, 2 * D_EXPERT), w_down,
      b_down.reshape(DEPTH, N_EXPERTS, 1, D_MODEL))


def _combine_kernel(x1_ref, mod_ref, gw_ref, y4_ref, o_ref):
    gw = gw_ref[...]
    y = gw[:, 0:1] * y4_ref[:, 0:D_MODEL]
    for kk in range(1, TOP_K):
        y = y + gw[:, kk:kk + 1] * y4_ref[:, kk * D_MODEL:(kk + 1) * D_MODEL]
    o_ref[...] = x1_ref[...] + mod_ref[0][5:6] * y


def _combine(x1, mod_l, gatew, y4, n_ctx_tiles, tiles_per_seq):
    n_tok = x1.shape[0]
    y4 = y4.reshape(y4.shape[0] // TOP_K, TOP_K * D_MODEL)
    row = lambda i: (i, 0)
    modrow = lambda i: (_mod_row_of_tile(i, n_ctx_tiles, tiles_per_seq), 0, 0)
    return pl.pallas_call(
        _combine_kernel,
        grid=(n_tok // TM,),
        in_specs=[
            pl.BlockSpec((TM, D_MODEL), row),
            pl.BlockSpec((1, 6, D_MODEL), modrow),
            pl.BlockSpec((TM, LANES), row),
            pl.BlockSpec((TM, TOP_K * D_MODEL), row),
        ],
        out_specs=pl.BlockSpec((TM, D_MODEL), row),
        out_shape=jax.ShapeDtypeStruct((n_tok, D_MODEL), F32),
        compiler_params=_cparams(1),
        name="moe_combine",
    )(x1, mod_l, gatew, y4)


def _rope_tables(s_len):
    rows = s_len // GRID_W
    row = jnp.repeat(jnp.arange(rows, dtype=F32), GRID_W)
    col = jnp.tile(jnp.arange(GRID_W, dtype=F32), rows)
    n_freq = HEAD_DIM // 4
    inv = ROPE_BASE ** (-jnp.arange(n_freq, dtype=F32) / n_freq)
    ang_r = row[:, None] * inv
    ang_c = col[:, None] * inv
    ang = jnp.concatenate([ang_r, ang_r, ang_c, ang_c], axis=1)
    sign = jnp.tile(jnp.concatenate([-jnp.ones((n_freq,), F32), jnp.ones((n_freq,), F32)]), 2)
    cos = jnp.tile(jnp.cos(ang), (1, 2))
    sin = jnp.tile(jnp.sin(ang) * sign, (1, 2))
    cos = jnp.concatenate([jnp.ones((TM, LANES), F32), cos], axis=0)
    sin = jnp.concatenate([jnp.zeros((TM, LANES), F32), sin], axis=0)
    return cos, sin


def kernel(x_prompt, x_sample, c, cache_k, cache_v, c_ctx, w_ada, b_ada, norm_mix, norm_ffn, w_in, q_norm, k_norm, lam_qk, sub_norm, w_attn_out, conv_w, conv_b, conv_norm_g, conv_norm_b, w_conv_out, w_out, w_router, b_router, w_gate_up, b_gate_up, w_down, b_down):
    b_p, s_p, _ = x_prompt.shape
    b_s, s_s, _ = x_sample.shape
    past = cache_k.shape[2]
    n_ctx = b_p * s_p
    n_tok = n_ctx + b_s * s_s
    n_ctx_tiles = n_ctx // TM
    tiles_per_seq = s_s // TM
    assert s_p == TM and s_s % TMO == 0 and n_ctx % s_s == 0 and s_s % GRID_W == 0
    assert (n_tok * TOP_K) % INV_CHUNK == 0 and (n_tok * TOP_K) % EBLK == 0

    x = jnp.concatenate([x_prompt.reshape(n_ctx, D_MODEL), x_sample.reshape(b_s * s_s, D_MODEL)], axis=0)
    n_cond = 8 * pl.cdiv(1 + b_s, 8)
    cond = jnp.zeros((n_cond, D_MODEL), F32).at[0].set(c_ctx).at[1:1 + b_s].set(c)
    mod = _ada_all_layers(cond, w_ada.astype(BF16), b_ada).reshape(DEPTH, n_cond, 6, D_MODEL)

    w_in_bf = w_in.astype(BF16)
    wa_bf = w_attn_out.astype(BF16)
    wc_bf = w_conv_out.astype(BF16)
    wo_bf = w_out.astype(BF16)
    ck = cache_k.reshape(b_s, DEPTH, past, ATTN_WIDTH)
    cv = cache_v.reshape(b_s, DEPTH, past, ATTN_WIDTH)
    cos_t, sin_t = _rope_tables(s_s)
    grp = jnp.arange(ATTN_WIDTH, dtype=jnp.int32) // HEAD_DIM
    gsum = (grp[:, None] == grp[None, :]).astype(BF16)
    tri = (jnp.arange(TM)[:, None] > jnp.arange(TM)[None, :]).astype(BF16)
    n_blk = (n_tok * TOP_K) // EBLK + N_EXPERTS
    n_rows = n_blk * EBLK
    blk_row0 = jnp.arange(n_blk, dtype=jnp.int32) * EBLK
    inv_init = n_tok * TOP_K + jnp.arange(n_rows, dtype=jnp.int32)

    new_k, new_v = [], []
    for l in range(DEPTH):
        lam_init = 0.8 - 0.6 * math.exp(-0.3 * l)
        q, k, v, zpre, gates = _inproj(
            x, mod[l], norm_mix[l].reshape(1, D_MODEL), w_in_bf, l,
            jnp.tile(q_norm[l], ATTN_WIDTH // HEAD_DIM).reshape(1, ATTN_WIDTH),
            jnp.tile(k_norm[l], ATTN_WIDTH // HEAD_DIM).reshape(1, ATTN_WIDTH),
            gsum, cos_t, sin_t, n_ctx_tiles, tiles_per_seq)
        new_k.append(k[:n_ctx].reshape(b_p, s_p, N_HEADS, 2 * HEAD_DIM))
        new_v.append(v[:n_ctx].reshape(b_p, s_p, N_HEADS, 2 * HEAD_DIM))
        sub_l = sub_norm[l].reshape(1, LANES)
        o_p = _attention(q, k, v, lam_qk[l], sub_l, lam_init, b_p, s_p, 0)
        o_s = _attention(q, k, v, lam_qk[l], sub_l, lam_init, b_s, s_s, n_ctx // s_s, ctx=(ck, cv, l))
        conv_args = (conv_w[l], conv_b[l].reshape(1, CONV_WIDTH), conv_norm_g[l].reshape(1, CONV_WIDTH),
                     conv_norm_b[l].reshape(1, CONV_WIDTH))
        zc_p = _conv_module(zpre, *conv_args, b_p, s_p, 0)
        zc_s = _conv_module(zpre, *conv_args, b_s, s_s, n_ctx // s_s)
        wr_pad = jnp.zeros((D_MODEL, LANES), F32).at[:, :N_EXPERTS].set(w_router[l])
        wr_hi = wr_pad.astype(BF16)
        wr_lo = (wr_pad - wr_hi.astype(F32)).astype(BF16)
        br_pad = jnp.full((1, LANES), NEG_BIG, F32).at[0, :N_EXPERTS].set(b_router[l])
        x1, h2, top_e, gatew, rank, counts = _outproj(
            x, o_p, o_s, zc_p, zc_s, gates, mod[l], wa_bf, wc_bf, wo_bf, l,
            norm_ffn[l].reshape(1, D_MODEL), wr_hi, wr_lo, br_pad, tri, s_s)

        cnt = counts[0, :N_EXPERTS]
        padded = (cnt + EBLK - 1) // EBLK * EBLK
        pend = jnp.cumsum(padded)
        pstart = pend - padded
        block_e = jnp.minimum(jnp.sum((pend[None, :] <= blk_row0[:, None]).astype(jnp.int32), axis=1),
                              N_EXPERTS - 1)
        n_used = (pend[-1] // EBLK).astype(jnp.int32).reshape(1)

        inv = _invert(top_e[:, :TOP_K].reshape(n_tok * TOP_K), rank[:, :TOP_K].reshape(n_tok * TOP_K),
                      pstart, inv_init)
        y4 = _experts(h2, inv, block_e, n_used, w_gate_up, b_gate_up, w_down, b_down, l)
        x = _combine(x1, mod[l], gatew, y4, n_ctx_tiles, tiles_per_seq)

    y_prompt = x[:n_ctx].reshape(b_p, s_p, D_MODEL)
    y_sample = x[n_ctx:].reshape(b_s, s_s, D_MODEL)
    return (y_prompt, y_sample, jnp.stack(new_k, axis=1), jnp.stack(new_v, axis=1))
```

```python
import functools
import math

import jax
import jax.numpy as jnp
from jax import lax
from jax.experimental import pallas as pl
from jax.experimental.pallas import tpu as pltpu

F32 = jnp.float32
BF16 = jnp.bfloat16

D_MODEL = 1024
DEPTH = 4
GRID_W = 64
ATTN_WIDTH = 512
N_HEADS = 4
HEAD_DIM = 64
CONV_WIDTH = 512
CONV_KERNEL = 31
N_EXPERTS = 32
TOP_K = 4
D_EXPERT = 1024
SWIGLU_LIMIT = 7.0
SWIGLU_ALPHA = 1.702
ROPE_BASE = 10000.0
EPS = 1e-6

LANES = 128
TM = 256
TMO = 512
EBLK = 256
INV_CHUNK = 4096
NBUF = 3
ADA_TN = 1024
CONV_CHUNK = 64
CONV_HALO = 16
NEG_BIG = -1e30
VMEM_LIMIT = 56 * 1024 * 1024


def _cparams(n_axes, vmem=None):
    return pltpu.CompilerParams(
        dimension_semantics=("arbitrary",) * n_axes,
        vmem_limit_bytes=VMEM_LIMIT if vmem is None else vmem)


def _mod_row_of_tile(i, n_ctx_tiles, tiles_per_seq):
    return jnp.where(i < n_ctx_tiles, 0, 1 + (i - n_ctx_tiles) // tiles_per_seq)


def _ada_kernel(cond_ref, w_ref, b_ref, o_ref):
    cnd = cond_ref[...]
    act = cnd * jax.nn.sigmoid(cnd)
    o_ref[0] = jnp.dot(act.astype(BF16), w_ref[0], preferred_element_type=F32) + b_ref[0]


def _ada_all_layers(cond, w_ada_bf, b_ada):
    n_rows = cond.shape[0]
    n_col = w_ada_bf.shape[2]
    return pl.pallas_call(
        _ada_kernel,
        grid=(DEPTH, n_col // ADA_TN),
        in_specs=[
            pl.BlockSpec((n_rows, D_MODEL), lambda l, j: (0, 0)),
            pl.BlockSpec((1, D_MODEL, ADA_TN), lambda l, j: (l, 0, j)),
            pl.BlockSpec((1, 1, ADA_TN), lambda l, j: (l, 0, j)),
        ],
        out_specs=pl.BlockSpec((1, n_rows, ADA_TN), lambda l, j: (l, 0, j)),
        out_shape=jax.ShapeDtypeStruct((DEPTH, n_rows, n_col), F32),
        compiler_params=_cparams(2),
        name="ada",
    )(cond, w_ada_bf, b_ada.reshape(DEPTH, 1, n_col))


def _rms_mod(x, g, shift, scale):
    y = x * lax.rsqrt(jnp.mean(x * x, axis=-1, keepdims=True) + EPS) * g
    return y * (1.0 + scale) + shift


def _group_mean_sq(t, gsum):
    s = t * t
    hi = s.astype(BF16)
    lo = (s - hi.astype(F32)).astype(BF16)
    tot = jnp.dot(hi, gsum, preferred_element_type=F32) + jnp.dot(lo, gsum, preferred_element_type=F32)
    return tot * (1.0 / HEAD_DIM)


def _rope_slab(x, cos, sgn_sin, first_half):
    partner = jnp.where(first_half, pltpu.roll(x, LANES - 16, 1), pltpu.roll(x, 16, 1))
    return x * cos + partner * sgn_sin


def _inproj_kernel(x_ref, mod_ref, gmix_ref, w_ref, qn_ref, kn_ref, gsum_ref, cos_ref, sin_ref,
                   q_ref, k_ref, v_ref, z_ref, gate_ref):
    m = mod_ref[0]
    h = _rms_mod(x_ref[...], gmix_ref[...], m[0:1], m[1:2]).astype(BF16)
    aw = ATTN_WIDTH
    q = jnp.dot(h, w_ref[:, 0:aw], preferred_element_type=F32)
    k = jnp.dot(h, w_ref[:, aw:2 * aw], preferred_element_type=F32)
    gsum = gsum_ref[...]
    q = q * lax.rsqrt(_group_mean_sq(q, gsum) + EPS) * qn_ref[...]
    k = k * lax.rsqrt(_group_mean_sq(k, gsum) + EPS) * kn_ref[...]
    cos = cos_ref[...]
    sin = sin_ref[...]
    lane = lax.broadcasted_iota(jnp.int32, (TM, LANES), 1)
    first_half = (lane & 16) == 0
    for hd in range(N_HEADS):
        sl = slice(hd * LANES, (hd + 1) * LANES)
        q_ref[:, sl] = (_rope_slab(q[:, sl], cos, sin, first_half) * (HEAD_DIM ** -0.5)).astype(BF16)
        k_ref[:, sl] = _rope_slab(k[:, sl], cos, sin, first_half)
    v_ref[...] = jnp.dot(h, w_ref[:, 2 * aw:3 * aw], preferred_element_type=F32)
    c0 = 3 * aw
    a = jnp.dot(h, w_ref[:, c0:c0 + CONV_WIDTH], preferred_element_type=F32)
    g = jnp.dot(h, w_ref[:, c0 + CONV_WIDTH:c0 + 2 * CONV_WIDTH], preferred_element_type=F32)
    z_ref[...] = a * jax.nn.sigmoid(g)
    c1 = c0 + 2 * CONV_WIDTH
    gl = jnp.dot(h, w_ref[:, c1:c1 + 2 * D_MODEL], preferred_element_type=F32)
    gate_ref[...] = jax.nn.sigmoid(gl).astype(BF16)


def _inproj(x, mod_l, gmix, w_in_bf, l, qn, kn, gsum, cos_t, sin_t, n_ctx_tiles, tiles_per_seq):
    n_tok = x.shape[0]
    in_w = w_in_bf.shape[2]
    row = lambda i: (i, 0)
    modrow = lambda i: (_mod_row_of_tile(i, n_ctx_tiles, tiles_per_seq), 0, 0)
    ropeblk = lambda i: (jnp.where(i < n_ctx_tiles, 0, 1 + (i - n_ctx_tiles) % tiles_per_seq), 0)
    const2 = lambda i: (0, 0)
    return pl.pallas_call(
        _inproj_kernel,
        grid=(n_tok // TM,),
        in_specs=[
            pl.BlockSpec((TM, D_MODEL), row),
            pl.BlockSpec((1, 6, D_MODEL), modrow),
            pl.BlockSpec((1, D_MODEL), const2),
            pl.BlockSpec((None, D_MODEL, in_w), lambda i: (l, 0, 0)),
            pl.BlockSpec((1, ATTN_WIDTH), const2),
            pl.BlockSpec((1, ATTN_WIDTH), const2),
            pl.BlockSpec((ATTN_WIDTH, ATTN_WIDTH), const2),
            pl.BlockSpec((TM, LANES), ropeblk),
            pl.BlockSpec((TM, LANES), ropeblk),
        ],
        out_specs=[
            pl.BlockSpec((TM, ATTN_WIDTH), row),
            pl.BlockSpec((TM, ATTN_WIDTH), row),
            pl.BlockSpec((TM, ATTN_WIDTH), row),
            pl.BlockSpec((TM, CONV_WIDTH), row),
            pl.BlockSpec((TM, 2 * D_MODEL), row),
        ],
        out_shape=[
            jax.ShapeDtypeStruct((n_tok, ATTN_WIDTH), BF16),
            jax.ShapeDtypeStruct((n_tok, ATTN_WIDTH), F32),
            jax.ShapeDtypeStruct((n_tok, ATTN_WIDTH), F32),
            jax.ShapeDtypeStruct((n_tok, CONV_WIDTH), F32),
            jax.ShapeDtypeStruct((n_tok, 2 * D_MODEL), BF16),
        ],
        compiler_params=_cparams(1),
        name="inproj",
    )(x, mod_l, gmix, w_in_bf, qn, kn, gsum, cos_t, sin_t)


def _attn_kernel(*refs, lam_init, s_q, s_ctx, q_chunk):
    if s_ctx:
        q_ref, k_ref, v_ref, ck_ref, cv_ref, lam_ref, sub_ref, o_ref, kc_ref, vc_ref = refs
        kc_ref[0:s_ctx, :] = ck_ref[...].astype(BF16)
        vc_ref[0:s_ctx, :] = cv_ref[...].astype(BF16)
    else:
        q_ref, k_ref, v_ref, lam_ref, sub_ref, o_ref, kc_ref, vc_ref = refs
    kc_ref[s_ctx:s_ctx + s_q, :] = k_ref[...].astype(BF16)
    vc_ref[s_ctx:s_ctx + s_q, :] = v_ref[...].astype(BF16)
    lq = lam_ref[...]
    lam = (jnp.exp(jnp.sum(lq[0:1] * lq[1:2], axis=-1, keepdims=True))
           - jnp.exp(jnp.sum(lq[2:3] * lq[3:4], axis=-1, keepdims=True)) + lam_init)
    lane = lax.broadcasted_iota(jnp.int32, (q_chunk, LANES), 1)
    map0 = lane < HEAD_DIM
    sub = sub_ref[...]
    nt = (((1,), (1,)), ((), ()))

    def chunk(c, carry):
        r0 = pl.multiple_of(c * q_chunk, q_chunk)
        q = q_ref[pl.ds(r0, q_chunk), :]
        kc = kc_ref[...]
        zero = jnp.zeros_like(q)
        probs = []
        for mp in range(2):
            qm = jnp.where(map0, q, zero) if mp == 0 else jnp.where(map0, zero, q)
            s = lax.dot_general(qm, kc, nt, preferred_element_type=F32)
            e = jnp.exp(s - jnp.max(s, axis=-1, keepdims=True))
            probs.append(e / jnp.sum(e, axis=-1, keepdims=True))
        w = probs[0] - lam * probs[1]
        o = jnp.dot(w.astype(BF16), vc_ref[...], preferred_element_type=F32)
        o = o * lax.rsqrt(jnp.mean(o * o, axis=-1, keepdims=True) + EPS) * sub * (1.0 - lam_init)
        o_ref[pl.ds(r0, q_chunk), :] = o.astype(BF16)
        return carry

    lax.fori_loop(0, s_q // q_chunk, chunk, 0)


def _attention(q, k, v, lam_qk_l, sub_l, lam_init, n_batch, s_q, row_blk0, ctx=None):
    s_ctx = 0 if ctx is None else ctx[0].shape[2]
    own = pl.BlockSpec((s_q, LANES), lambda b, h: (row_blk0 + b, h))
    in_specs = [own, own, own]
    args = [q, k, v]
    if ctx is not None:
        ck, cv, l = ctx
        cspec = pl.BlockSpec((None, None, s_ctx, LANES), lambda b, h: (b, l, 0, h))
        in_specs += [cspec, cspec]
        args += [ck, cv]
    in_specs += [pl.BlockSpec((4, HEAD_DIM), lambda b, h: (0, 0)),
                 pl.BlockSpec((1, LANES), lambda b, h: (0, 0))]
    args += [lam_qk_l, sub_l]
    return pl.pallas_call(
        functools.partial(_attn_kernel, lam_init=lam_init, s_q=s_q, s_ctx=s_ctx, q_chunk=min(s_q, 256)),
        grid=(n_batch, N_HEADS),
        in_specs=in_specs,
        out_specs=pl.BlockSpec((s_q, LANES), lambda b, h: (b, h)),
        out_shape=jax.ShapeDtypeStruct((n_batch * s_q, ATTN_WIDTH), BF16),
        scratch_shapes=[pltpu.VMEM((s_ctx + s_q, LANES), BF16), pltpu.VMEM((s_ctx + s_q, LANES), BF16)],
        compiler_params=_cparams(2),
        name="attn_ctx" if ctx is not None else "attn",
    )(*args)


def _conv_kernel(z_ref, w_ref, b_ref, g_ref, beta_ref, o_ref, pad_ref, *, s_len):
    halo = jnp.zeros((CONV_HALO, CONV_WIDTH), F32)
    pad_ref[0:CONV_HALO, :] = halo
    pad_ref[CONV_HALO:CONV_HALO + s_len, :] = z_ref[...]
    pad_ref[CONV_HALO + s_len:2 * CONV_HALO + s_len, :] = halo
    bias = b_ref[...]
    gam = g_ref[...]
    beta = beta_ref[...]
    off0 = CONV_HALO - CONV_KERNEL // 2

    def chunk(c, carry):
        r0 = pl.multiple_of(c * CONV_CHUNK, CONV_CHUNK)
        slabs = []
        for s0 in range(0, CONV_WIDTH, LANES):
            sl = slice(s0, s0 + LANES)
            win = pad_ref[pl.ds(r0, CONV_CHUNK + 2 * CONV_HALO), sl]
            part = jnp.broadcast_to(bias[:, sl], (CONV_CHUNK, LANES))
            for res in range(8):
                shifted = win[res:res + CONV_CHUNK + 2 * CONV_HALO - 8]
                for a8 in range(0, 2 * CONV_HALO, 8):
                    j = a8 + res - off0
                    if 0 <= j < CONV_KERNEL:
                        part = part + shifted[a8:a8 + CONV_CHUNK] * w_ref[j:j + 1, sl]
            slabs.append(part)
        acc = jnp.concatenate(slabs, axis=-1)
        mu = jnp.mean(acc, axis=-1, keepdims=True)
        d = acc - mu
        var = jnp.mean(d * d, axis=-1, keepdims=True)
        y = d * lax.rsqrt(var + EPS) * gam + beta
        o_ref[pl.ds(r0, CONV_CHUNK), :] = (y * jax.nn.sigmoid(y)).astype(BF16)
        return carry

    lax.fori_loop(0, s_len // CONV_CHUNK, chunk, 0)


def _conv_module(z, conv_w_l, conv_b_l, g_l, beta_l, n_batch, s_len, row_blk0):
    const2 = lambda b: (0, 0)
    return pl.pallas_call(
        functools.partial(_conv_kernel, s_len=s_len),
        grid=(n_batch,),
        in_specs=[
            pl.BlockSpec((s_len, CONV_WIDTH), lambda b: (row_blk0 + b, 0)),
            pl.BlockSpec((CONV_KERNEL, CONV_WIDTH), const2),
            pl.BlockSpec((1, CONV_WIDTH), const2),
            pl.BlockSpec((1, CONV_WIDTH), const2),
            pl.BlockSpec((1, CONV_WIDTH), const2),
        ],
        out_specs=pl.BlockSpec((s_len, CONV_WIDTH), lambda b: (b, 0)),
        out_shape=jax.ShapeDtypeStruct((n_batch * s_len, CONV_WIDTH), BF16),
        scratch_shapes=[pltpu.VMEM((s_len + 2 * CONV_HALO, CONV_WIDTH), F32)],
        compiler_params=_cparams(1),
        name="conv",
    )(z, conv_w_l, conv_b_l, g_l, beta_l)


def _route_rows(logits, seen, tri):
    rows = logits.shape[0]
    lane = lax.broadcasted_iota(jnp.int32, (rows, LANES), 1)
    lane_f = lane.astype(F32)
    work = logits
    sel_masks, top_vals, top_idx = [], [], []
    for _ in range(TOP_K):
        mx = jnp.max(work, axis=-1, keepdims=True)
        idx = jnp.min(jnp.where(work == mx, lane_f, float(LANES)), axis=-1, keepdims=True)
        sel = lane_f == idx
        sel_masks.append(sel)
        top_vals.append(mx)
        top_idx.append(idx)
        work = jnp.where(sel, NEG_BIG * 2.0, work)
    exps = [jnp.exp(v - top_vals[0]) for v in top_vals]
    denom = exps[0] + exps[1] + exps[2] + exps[3]
    onehot = jnp.zeros((rows, LANES), F32)
    for sel in sel_masks:
        onehot = onehot + jnp.where(sel, 1.0, 0.0)
    base = jnp.dot(tri, onehot.astype(BF16), preferred_element_type=F32) + seen
    te = jnp.zeros((rows, LANES), F32)
    gw = jnp.zeros((rows, LANES), F32)
    rk = jnp.zeros((rows, LANES), F32)
    for kk in range(TOP_K):
        r = jnp.sum(jnp.where(sel_masks[kk], base, 0.0), axis=-1, keepdims=True)
        col = lane == kk
        te = jnp.where(col, top_idx[kk], te)
        gw = jnp.where(col, exps[kk] / denom, gw)
        rk = jnp.where(col, r, rk)
    return te.astype(jnp.int32), gw, rk.astype(jnp.int32), jnp.sum(onehot, axis=0, keepdims=True)


def _outproj_kernel(x_ref, op_ref, os_ref, zp_ref, zs_ref, gate_ref, mod_ref, wa_ref, wc_ref, wo_ref,
                    gffn_ref, wrh_ref, wrl_ref, br_ref, tri_ref,
                    x1_ref, h2_ref, te_ref, gw_ref, rank_ref, cnt_ref, carry_ref, *, n_ctx_tiles):
    i = pl.program_id(0)

    @pl.when(i == 0)
    def _():
        carry_ref[...] = jnp.zeros_like(carry_ref)

    m = mod_ref[0]
    is_ctx = i < n_ctx_tiles
    seen = carry_ref[...]
    for g0 in range(0, TMO, TM):
        rs = slice(g0, g0 + TM)
        o = jnp.where(is_ctx, op_ref[rs, :], os_ref[rs, :])
        zc = jnp.where(is_ctx, zp_ref[rs, :], zs_ref[rs, :])
        attn_d = jnp.dot(o, wa_ref[...], preferred_element_type=F32)
        conv_d = jnp.dot(zc, wc_ref[...], preferred_element_type=F32)
        y = (gate_ref[rs, 0:D_MODEL].astype(F32) * attn_d
             + gate_ref[rs, D_MODEL:2 * D_MODEL].astype(F32) * conv_d)
        mix = jnp.dot(y.astype(BF16), wo_ref[...], preferred_element_type=F32)
        x1 = x_ref[rs, :] + m[2:3] * mix
        x1_ref[rs, :] = x1
        h2 = _rms_mod(x1, gffn_ref[...], m[3:4], m[4:5])
        h2_ref[rs, :] = h2
        h_hi = h2.astype(BF16)
        h_lo = (h2 - h_hi.astype(F32)).astype(BF16)
        logits = (jnp.dot(h_hi, wrh_ref[...], preferred_element_type=F32)
                  + (jnp.dot(h_lo, wrh_ref[...], preferred_element_type=F32)
                     + jnp.dot(h_hi, wrl_ref[...], preferred_element_type=F32))) + br_ref[...]
        te, gw, rk, cnt = _route_rows(logits, seen, tri_ref[...])
        te_ref[rs, :] = te
        gw_ref[rs, :] = gw
        rank_ref[rs, :] = rk
        seen = seen + cnt
    carry_ref[...] = seen
    cnt_ref[...] = jnp.broadcast_to(seen, cnt_ref.shape).astype(jnp.int32)


def _outproj(x, o_p, o_s, zc_p, zc_s, gates, mod_l, wa_bf, wc_bf, wo_bf, l, gffn, wr_hi, wr_lo, br_pad,
             tri, s_lat):
    n_tok = x.shape[0]
    n_ctx_tiles = o_p.shape[0] // TMO
    n_lat_tiles = o_s.shape[0] // TMO
    row = lambda i: (i, 0)
    ctxrow = lambda i: (jnp.minimum(i, n_ctx_tiles - 1), 0)
    latrow = lambda i: (jnp.clip(i - n_ctx_tiles, 0, n_lat_tiles - 1), 0)
    const2 = lambda i: (0, 0)
    lay3 = lambda i: (l, 0, 0)
    modrow = lambda i: (_mod_row_of_tile(i, n_ctx_tiles, s_lat // TMO), 0, 0)
    return pl.pallas_call(
        functools.partial(_outproj_kernel, n_ctx_tiles=n_ctx_tiles),
        grid=(n_tok // TMO,),
        in_specs=[
            pl.BlockSpec((TMO, D_MODEL), row),
            pl.BlockSpec((TMO, ATTN_WIDTH), ctxrow),
            pl.BlockSpec((TMO, ATTN_WIDTH), latrow),
            pl.BlockSpec((TMO, CONV_WIDTH), ctxrow),
            pl.BlockSpec((TMO, CONV_WIDTH), latrow),
            pl.BlockSpec((TMO, 2 * D_MODEL), row),
            pl.BlockSpec((1, 6, D_MODEL), modrow),
            pl.BlockSpec((None, ATTN_WIDTH, D_MODEL), lay3),
            pl.BlockSpec((None, CONV_WIDTH, D_MODEL), lay3),
            pl.BlockSpec((None, D_MODEL, D_MODEL), lay3),
            pl.BlockSpec((1, D_MODEL), const2),
            pl.BlockSpec((D_MODEL, LANES), const2),
            pl.BlockSpec((D_MODEL, LANES), const2),
            pl.BlockSpec((1, LANES), const2),
            pl.BlockSpec((TM, TM), const2),
        ],
        out_specs=[
            pl.BlockSpec((TMO, D_MODEL), row),
            pl.BlockSpec((TMO, D_MODEL), row),
            pl.BlockSpec((TMO, LANES), row),
            pl.BlockSpec((TMO, LANES), row),
            pl.BlockSpec((TMO, LANES), row),
            pl.BlockSpec((8, LANES), const2),
        ],
        out_shape=[
            jax.ShapeDtypeStruct((n_tok, D_MODEL), F32),
            jax.ShapeDtypeStruct((n_tok, D_MODEL), F32),
            jax.ShapeDtypeStruct((n_tok, LANES), jnp.int32),
            jax.ShapeDtypeStruct((n_tok, LANES), F32),
            jax.ShapeDtypeStruct((n_tok, LANES), jnp.int32),
            jax.ShapeDtypeStruct((8, LANES), jnp.int32),
        ],
        scratch_shapes=[pltpu.VMEM((1, LANES), F32)],
        compiler_params=_cparams(1),
        name="outproj_router",
    )(x, o_p, o_s, zc_p, zc_s, gates, mod_l, wa_bf, wc_bf, wo_bf, gffn, wr_hi, wr_lo, br_pad, tri)


def _invert_kernel(te_ref, rk_ref, pstart_ref, init_ref, inv_ref, sem, *, chunks_per_slot):
    kk = pl.program_id(0)
    c = pl.program_id(1)

    @pl.when(jnp.logical_and(kk == 0, c == 0))
    def _():
        cp = pltpu.make_async_copy(init_ref, inv_ref, sem)
        cp.start()
        cp.wait()

    base = (kk * chunks_per_slot + c) * INV_CHUNK

    def body(a, carry):
        inv_ref[pstart_ref[te_ref[a]] + rk_ref[a]] = base + a
        return carry

    lax.fori_loop(0, INV_CHUNK, body, 0, unroll=8)


def _invert(te_flat, rk_flat, pstart, inv_init, n_tok, tok_pad):
    chunks_per_slot = tok_pad // INV_CHUNK
    chunk = lambda kk, c: (kk * chunks_per_slot + c,)
    return pl.pallas_call(
        functools.partial(_invert_kernel, chunks_per_slot=chunks_per_slot),
        grid=(TOP_K, n_tok // INV_CHUNK),
        in_specs=[
            pl.BlockSpec((INV_CHUNK,), chunk, memory_space=pltpu.SMEM),
            pl.BlockSpec((INV_CHUNK,), chunk, memory_space=pltpu.SMEM),
            pl.BlockSpec(memory_space=pltpu.SMEM),
            pl.BlockSpec(memory_space=pl.ANY),
        ],
        out_specs=pl.BlockSpec(memory_space=pltpu.SMEM),
        out_shape=jax.ShapeDtypeStruct(inv_init.shape, jnp.int32),
        scratch_shapes=[pltpu.SemaphoreType.DMA(())],
        compiler_params=_cparams(2),
        name="moe_invert",
    )(te_flat, rk_flat, pstart, inv_init)


def _expert_kernel(be_ref, nused_ref, inv_out_ref, inv_in_ref, h_ref, wgu_ref, bgu_ref, wd_ref, bd_ref,
                   y4_ref, wgu_bf, wd_bf, xb0, xb1, xb2, yb0, yb1, yb2, gsem, ssem,
                   *, n_tok, tok_pad, n_blk, bogus_row0):
    j = pl.program_id(0)
    n_used = nused_ref[0]
    xbuf = (xb0, xb1, xb2)
    ybuf = (yb0, yb1, yb2)

    def gather_copy(r, slot):
        tok = jnp.minimum(inv_in_ref[r] & (tok_pad - 1), n_tok - 1)
        return pltpu.make_async_copy(h_ref.at[pl.ds(tok, 1), :], xbuf[slot].at[pl.ds(r, 1), :],
                                     gsem.at[slot])

    def scatter_copy(r, slot, real):
        row = jnp.where(real, inv_out_ref[r], bogus_row0 + r)
        return pltpu.make_async_copy(ybuf[slot].at[pl.ds(r, 1), :], y4_ref.at[pl.ds(row, 1), :],
                                     ssem.at[slot])

    def wait_gather(slot):
        pltpu.make_async_copy(h_ref.at[pl.ds(0, EBLK), :], xbuf[slot], gsem.at[slot]).wait()

    def wait_scatter(slot):
        pltpu.make_async_copy(ybuf[slot], y4_ref.at[pl.ds(0, EBLK), :], ssem.at[slot]).wait()

    def rolled(start_one):
        def body(r, carry):
            start_one(r)
            return carry
        lax.fori_loop(0, EBLK, body, 0)

    @pl.when(j == 0)
    def _():
        for yb in ybuf:
            yb[...] = jnp.zeros_like(yb)
        rolled(lambda r: gather_copy(r, 0).start())

    @pl.when(j == 1)
    def _():
        rolled(lambda r: gather_copy(r, 1).start())

    blk = j - 2
    is_compute = jnp.logical_and(j >= 2, j <= n_used + 1)
    e_cur = be_ref[jnp.clip(blk, 0, n_blk - 1)]
    e_prev = be_ref[jnp.clip(blk - 1, 0, n_blk - 1)]

    @pl.when(jnp.logical_and(is_compute, jnp.logical_or(blk == 0, e_cur != e_prev)))
    def _():
        wgu_bf[...] = wgu_ref[...].astype(BF16)
        wd_bf[...] = wd_ref[...].astype(BF16)

    for slot in range(NBUF):
        nxt = (slot + 1) % NBUF
        prv = (slot + 2) % NBUF

        @pl.when(jnp.logical_and(is_compute, blk % NBUF == slot))
        def _():
            wait_gather(slot)

            @pl.when(blk >= 2)
            def _():
                wait_scatter(slot)

            for r in range(EBLK):
                gather_copy(r, prv).start()
            for r in range(EBLK):
                scatter_copy(r, prv, blk >= 1).start()
            x = xbuf[slot][...].astype(BF16)
            gu = jnp.dot(x, wgu_bf[...], preferred_element_type=F32) + bgu_ref[...]
            g = jnp.minimum(gu[:, 0:D_EXPERT], SWIGLU_LIMIT)
            u = jnp.clip(gu[:, D_EXPERT:2 * D_EXPERT], -SWIGLU_LIMIT, SWIGLU_LIMIT)
            act = (u + 1.0) * (g * jax.nn.sigmoid(SWIGLU_ALPHA * g))
            ybuf[slot][...] = (jnp.dot(act.astype(BF16), wd_bf[...], preferred_element_type=F32)
                               + bd_ref[...])

        @pl.when(jnp.logical_and(j == n_used + 2, (n_used - 1) % NBUF == slot))
        def _():
            wait_gather(nxt)
            wait_gather(prv)
            wait_scatter(prv)

            @pl.when(n_used >= 2)
            def _():
                wait_scatter(nxt)

            rolled(lambda r: scatter_copy(r, slot, True).start())
            wait_scatter(slot)


def _experts(h2, inv, block_e, n_used, w_gate_up, b_gate_up, w_down, b_down, l, tok_pad):
    n_tok = h2.shape[0]
    n_rows = inv.shape[0]
    n_blk = n_rows // EBLK
    bogus_row0 = TOP_K * tok_pad + n_rows
    wblk = lambda j, be, nu: (l, be[jnp.clip(j - 2, 0, nu[0] - 1)], 0, 0)
    tile = pltpu.VMEM((EBLK, D_MODEL), F32)
    grid_spec = pltpu.PrefetchScalarGridSpec(
        num_scalar_prefetch=2,
        grid=(n_blk + 3,),
        in_specs=[
            pl.BlockSpec((EBLK,), lambda j, be, nu: (jnp.clip(j - 3, 0, n_blk - 1),),
                         memory_space=pltpu.SMEM),
            pl.BlockSpec((EBLK,), lambda j, be, nu: (jnp.minimum(j, n_blk - 1),),
                         memory_space=pltpu.SMEM),
            pl.BlockSpec(memory_space=pl.ANY),
            pl.BlockSpec((None, None, D_MODEL, 2 * D_EXPERT), wblk),
            pl.BlockSpec((None, None, 1, 2 * D_EXPERT), wblk),
            pl.BlockSpec((None, None, D_EXPERT, D_MODEL), wblk),
            pl.BlockSpec((None, None, 1, D_MODEL), wblk),
        ],
        out_specs=pl.BlockSpec(memory_space=pl.ANY),
        scratch_shapes=[
            pltpu.VMEM((D_MODEL, 2 * D_EXPERT), BF16),
            pltpu.VMEM((D_EXPERT, D_MODEL), BF16),
            tile, tile, tile, tile, tile, tile,
            pltpu.SemaphoreType.DMA((NBUF,)),
            pltpu.SemaphoreType.DMA((NBUF,)),
        ],
    )
    return pl.pallas_call(
        functools.partial(_expert_kernel, n_tok=n_tok, tok_pad=tok_pad, n_blk=n_blk,
                          bogus_row0=bogus_row0),
        grid_spec=grid_spec,
        out_shape=jax.ShapeDtypeStruct((bogus_row0 + EBLK, D_MODEL), F32),
        compiler_params=_cparams(1),
        name="moe_experts",
    )(block_e, n_used, inv, inv, h2, w_gate_up,
      b_gate_up.reshape(DEPTH, N_EXPERTS, 1, 2 * D_EXPERT), w_down,
      b_down.reshape(DEPTH, N_EXPERTS, 1, D_MODEL))


def _combine_kernel(x1_ref, mod_ref, gw_ref, *rest):
    y_refs, o_ref = rest[:TOP_K], rest[TOP_K]
    gw = gw_ref[...]
    y = gw[:, 0:1] * y_refs[0][...]
    for kk in range(1, TOP_K):
        y = y + gw[:, kk:kk + 1] * y_refs[kk][...]
    o_ref[...] = x1_ref[...] + mod_ref[0][5:6] * y


def _combine(x1, mod_l, gatew, y4, n_ctx_tiles, tiles_per_seq, tok_pad):
    n_tok = x1.shape[0]
    row = lambda i: (i, 0)
    modrow = lambda i: (_mod_row_of_tile(i, n_ctx_tiles, tiles_per_seq), 0, 0)

    def slot_rows(kk):
        return lambda i: (kk * (tok_pad // TM) + i, 0)

    return pl.pallas_call(
        _combine_kernel,
        grid=(n_tok // TM,),
        in_specs=[
            pl.BlockSpec((TM, D_MODEL), row),
            pl.BlockSpec((1, 6, D_MODEL), modrow),
            pl.BlockSpec((TM, LANES), row),
        ] + [pl.BlockSpec((TM, D_MODEL), slot_rows(kk)) for kk in range(TOP_K)],
        out_specs=pl.BlockSpec((TM, D_MODEL), row),
        out_shape=jax.ShapeDtypeStruct((n_tok, D_MODEL), F32),
        compiler_params=_cparams(1),
        name="moe_combine",
    )(x1, mod_l, gatew, *([y4] * TOP_K))


def _rope_tables(s_len):
    rows = s_len // GRID_W
    row = jnp.repeat(jnp.arange(rows, dtype=F32), GRID_W)
    col = jnp.tile(jnp.arange(GRID_W, dtype=F32), rows)
    n_freq = HEAD_DIM // 4
    inv = ROPE_BASE ** (-jnp.arange(n_freq, dtype=F32) / n_freq)
    ang_r = row[:, None] * inv
    ang_c = col[:, None] * inv
    ang = jnp.concatenate([ang_r, ang_r, ang_c, ang_c], axis=1)
    sign = jnp.tile(jnp.concatenate([-jnp.ones((n_freq,), F32), jnp.ones((n_freq,), F32)]), 2)
    cos = jnp.tile(jnp.cos(ang), (1, 2))
    sin = jnp.tile(jnp.sin(ang) * sign, (1, 2))
    cos = jnp.concatenate([jnp.ones((TM, LANES), F32), cos], axis=0)
    sin = jnp.concatenate([jnp.zeros((TM, LANES), F32), sin], axis=0)
    return cos, sin


def kernel(x_prompt, x_sample, c, cache_k, cache_v, c_ctx, w_ada, b_ada, norm_mix, norm_ffn, w_in, q_norm, k_norm, lam_qk, sub_norm, w_attn_out, conv_w, conv_b, conv_norm_g, conv_norm_b, w_conv_out, w_out, w_router, b_router, w_gate_up, b_gate_up, w_down, b_down):
    b_p, s_p, _ = x_prompt.shape
    b_s, s_s, _ = x_sample.shape
    past = cache_k.shape[2]
    n_ctx = b_p * s_p
    n_tok = n_ctx + b_s * s_s
    n_ctx_tiles = n_ctx // TM
    tiles_per_seq = s_s // TM
    assert s_p == TM and s_s % TMO == 0 and n_ctx % s_s == 0 and s_s % GRID_W == 0
    assert n_tok % INV_CHUNK == 0 and (n_tok * TOP_K) % EBLK == 0

    x = jnp.concatenate([x_prompt.reshape(n_ctx, D_MODEL), x_sample.reshape(b_s * s_s, D_MODEL)], axis=0)
    n_cond = 8 * pl.cdiv(1 + b_s, 8)
    cond = jnp.zeros((n_cond, D_MODEL), F32).at[0].set(c_ctx).at[1:1 + b_s].set(c)
    mod = _ada_all_layers(cond, w_ada.astype(BF16), b_ada).reshape(DEPTH, n_cond, 6, D_MODEL)

    w_in_bf = w_in.astype(BF16)
    wa_bf = w_attn_out.astype(BF16)
    wc_bf = w_conv_out.astype(BF16)
    wo_bf = w_out.astype(BF16)
    ck = cache_k.reshape(b_s, DEPTH, past, ATTN_WIDTH)
    cv = cache_v.reshape(b_s, DEPTH, past, ATTN_WIDTH)
    cos_t, sin_t = _rope_tables(s_s)
    grp = jnp.arange(ATTN_WIDTH, dtype=jnp.int32) // HEAD_DIM
    gsum = (grp[:, None] == grp[None, :]).astype(BF16)
    tri = (jnp.arange(TM)[:, None] > jnp.arange(TM)[None, :]).astype(BF16)
    n_blk = (n_tok * TOP_K) // EBLK + N_EXPERTS
    n_rows = n_blk * EBLK
    blk_row0 = jnp.arange(n_blk, dtype=jnp.int32) * EBLK
    tok_pad = pl.next_power_of_2(n_tok)
    inv_init = TOP_K * tok_pad + jnp.arange(n_rows, dtype=jnp.int32)

    new_k, new_v = [], []
    for l in range(DEPTH):
        lam_init = 0.8 - 0.6 * math.exp(-0.3 * l)
        q, k, v, zpre, gates = _inproj(
            x, mod[l], norm_mix[l].reshape(1, D_MODEL), w_in_bf, l,
            jnp.tile(q_norm[l], ATTN_WIDTH // HEAD_DIM).reshape(1, ATTN_WIDTH),
            jnp.tile(k_norm[l], ATTN_WIDTH // HEAD_DIM).reshape(1, ATTN_WIDTH),
            gsum, cos_t, sin_t, n_ctx_tiles, tiles_per_seq)
        new_k.append(k[:n_ctx].reshape(b_p, s_p, N_HEADS, 2 * HEAD_DIM))
        new_v.append(v[:n_ctx].reshape(b_p, s_p, N_HEADS, 2 * HEAD_DIM))
        sub_l = sub_norm[l].reshape(1, LANES)
        o_p = _attention(q, k, v, lam_qk[l], sub_l, lam_init, b_p, s_p, 0)
        o_s = _attention(q, k, v, lam_qk[l], sub_l, lam_init, b_s, s_s, n_ctx // s_s, ctx=(ck, cv, l))
        conv_args = (conv_w[l], conv_b[l].reshape(1, CONV_WIDTH), conv_norm_g[l].reshape(1, CONV_WIDTH),
                     conv_norm_b[l].reshape(1, CONV_WIDTH))
        zc_p = _conv_module(zpre, *conv_args, b_p, s_p, 0)
        zc_s = _conv_module(zpre, *conv_args, b_s, s_s, n_ctx // s_s)
        wr_pad = jnp.zeros((D_MODEL, LANES), F32).at[:, :N_EXPERTS].set(w_router[l])
        wr_hi = wr_pad.astype(BF16)
        wr_lo = (wr_pad - wr_hi.astype(F32)).astype(BF16)
        br_pad = jnp.full((1, LANES), NEG_BIG, F32).at[0, :N_EXPERTS].set(b_router[l])
        x1, h2, top_e, gatew, rank, counts = _outproj(
            x, o_p, o_s, zc_p, zc_s, gates, mod[l], wa_bf, wc_bf, wo_bf, l,
            norm_ffn[l].reshape(1, D_MODEL), wr_hi, wr_lo, br_pad, tri, s_s)

        cnt = counts[0, :N_EXPERTS]
        padded = (cnt + EBLK - 1) // EBLK * EBLK
        pend = jnp.cumsum(padded)
        pstart = pend - padded
        block_e = jnp.minimum(jnp.sum((pend[None, :] <= blk_row0[:, None]).astype(jnp.int32), axis=1),
                              N_EXPERTS - 1)
        n_used = (pend[-1] // EBLK).astype(jnp.int32).reshape(1)

        def slot_major(a):
            return jnp.pad(a[:, :TOP_K].T, ((0, 0), (0, tok_pad - n_tok))).reshape(TOP_K * tok_pad)

        inv = _invert(slot_major(top_e), slot_major(rank), pstart, inv_init, n_tok, tok_pad)
        y4 = _experts(h2, inv, block_e, n_used, w_gate_up, b_gate_up, w_down, b_down, l, tok_pad)
        x = _combine(x1, mod[l], gatew, y4, n_ctx_tiles, tiles_per_seq, tok_pad)

    y_prompt = x[:n_ctx].reshape(b_p, s_p, D_MODEL)
    y_sample = x[n_ctx:].reshape(b_s, s_s, D_MODEL)
    return (y_prompt, y_sample, jnp.stack(new_k, axis=1), jnp.stack(new_v, axis=1))
```

```python
import functools
import math

import jax
import jax.numpy as jnp
from jax import lax
from jax.experimental import pallas as pl
from jax.experimental.pallas import tpu as pltpu

F32 = jnp.float32
BF16 = jnp.bfloat16

D_MODEL = 1024
DEPTH = 4
GRID_W = 64
ATTN_WIDTH = 512
N_HEADS = 4
HEAD_DIM = 64
CONV_WIDTH = 512
CONV_KERNEL = 31
N_EXPERTS = 32
TOP_K = 4
D_EXPERT = 1024
SWIGLU_LIMIT = 7.0
SWIGLU_ALPHA = 1.702
ROPE_BASE = 10000.0
EPS = 1e-6

LANES = 128
TM = 256
TMO = 512
EBLK = 256
INV_CHUNK = 4096
NBUF = 3
ADA_TN = 1024
CONV_CHUNK = 64
CONV_HALO = 16
NEG_BIG = -1e30
VMEM_LIMIT = 56 * 1024 * 1024


def _cparams(n_axes, vmem=None):
    return pltpu.CompilerParams(
        dimension_semantics=("arbitrary",) * n_axes,
        vmem_limit_bytes=VMEM_LIMIT if vmem is None else vmem)


def _mod_row_of_tile(i, n_ctx_tiles, tiles_per_seq):
    return jnp.where(i < n_ctx_tiles, 0, 1 + (i - n_ctx_tiles) // tiles_per_seq)


def _ada_kernel(cond_ref, w_ref, b_ref, o_ref):
    cnd = cond_ref[...]
    act = cnd * jax.nn.sigmoid(cnd)
    o_ref[0] = jnp.dot(act.astype(BF16), w_ref[0], preferred_element_type=F32) + b_ref[0]


def _ada_all_layers(cond, w_ada_bf, b_ada):
    n_rows = cond.shape[0]
    n_col = w_ada_bf.shape[2]
    return pl.pallas_call(
        _ada_kernel,
        grid=(DEPTH, n_col // ADA_TN),
        in_specs=[
            pl.BlockSpec((n_rows, D_MODEL), lambda l, j: (0, 0)),
            pl.BlockSpec((1, D_MODEL, ADA_TN), lambda l, j: (l, 0, j)),
            pl.BlockSpec((1, 1, ADA_TN), lambda l, j: (l, 0, j)),
        ],
        out_specs=pl.BlockSpec((1, n_rows, ADA_TN), lambda l, j: (l, 0, j)),
        out_shape=jax.ShapeDtypeStruct((DEPTH, n_rows, n_col), F32),
        compiler_params=_cparams(2),
        name="ada",
    )(cond, w_ada_bf, b_ada.reshape(DEPTH, 1, n_col))


def _rms_mod(x, g, shift, scale):
    y = x * lax.rsqrt(jnp.mean(x * x, axis=-1, keepdims=True) + EPS) * g
    return y * (1.0 + scale) + shift


def _group_mean_sq(t, gsum):
    s = t * t
    hi = s.astype(BF16)
    lo = (s - hi.astype(F32)).astype(BF16)
    tot = jnp.dot(hi, gsum, preferred_element_type=F32) + jnp.dot(lo, gsum, preferred_element_type=F32)
    return tot * (1.0 / HEAD_DIM)


def _rope_slab(x, cos, sgn_sin, first_half):
    partner = jnp.where(first_half, pltpu.roll(x, LANES - 16, 1), pltpu.roll(x, 16, 1))
    return x * cos + partner * sgn_sin


def _inproj_kernel(x_ref, mod_ref, gmix_ref, w_ref, qn_ref, kn_ref, gsum_ref, cos_ref, sin_ref,
                   q_ref, k_ref, v_ref, z_ref, gate_ref):
    m = mod_ref[0]
    h = _rms_mod(x_ref[...], gmix_ref[...], m[0:1], m[1:2]).astype(BF16)
    aw = ATTN_WIDTH
    q = jnp.dot(h, w_ref[:, 0:aw], preferred_element_type=F32)
    k = jnp.dot(h, w_ref[:, aw:2 * aw], preferred_element_type=F32)
    gsum = gsum_ref[...]
    q = q * lax.rsqrt(_group_mean_sq(q, gsum) + EPS) * qn_ref[...]
    k = k * lax.rsqrt(_group_mean_sq(k, gsum) + EPS) * kn_ref[...]
    cos = cos_ref[...]
    sin = sin_ref[...]
    lane = lax.broadcasted_iota(jnp.int32, (TM, LANES), 1)
    first_half = (lane & 16) == 0
    for hd in range(N_HEADS):
        sl = slice(hd * LANES, (hd + 1) * LANES)
        q_ref[:, sl] = (_rope_slab(q[:, sl], cos, sin, first_half) * (HEAD_DIM ** -0.5)).astype(BF16)
        k_ref[:, sl] = _rope_slab(k[:, sl], cos, sin, first_half)
    v_ref[...] = jnp.dot(h, w_ref[:, 2 * aw:3 * aw], preferred_element_type=F32)
    c0 = 3 * aw
    a = jnp.dot(h, w_ref[:, c0:c0 + CONV_WIDTH], preferred_element_type=F32)
    g = jnp.dot(h, w_ref[:, c0 + CONV_WIDTH:c0 + 2 * CONV_WIDTH], preferred_element_type=F32)
    z_ref[...] = a * jax.nn.sigmoid(g)
    c1 = c0 + 2 * CONV_WIDTH
    gl = jnp.dot(h, w_ref[:, c1:c1 + 2 * D_MODEL], preferred_element_type=F32)
    gate_ref[...] = jax.nn.sigmoid(gl).astype(BF16)


def _inproj(x, mod_l, gmix, w_in_bf, l, qn, kn, gsum, cos_t, sin_t, n_ctx_tiles, tiles_per_seq):
    n_tok = x.shape[0]
    in_w = w_in_bf.shape[2]
    row = lambda i: (i, 0)
    modrow = lambda i: (_mod_row_of_tile(i, n_ctx_tiles, tiles_per_seq), 0, 0)
    ropeblk = lambda i: (jnp.where(i < n_ctx_tiles, 0, 1 + (i - n_ctx_tiles) % tiles_per_seq), 0)
    const2 = lambda i: (0, 0)
    return pl.pallas_call(
        _inproj_kernel,
        grid=(n_tok // TM,),
        in_specs=[
            pl.BlockSpec((TM, D_MODEL), row),
            pl.BlockSpec((1, 6, D_MODEL), modrow),
            pl.BlockSpec((1, D_MODEL), const2),
            pl.BlockSpec((None, D_MODEL, in_w), lambda i: (l, 0, 0)),
            pl.BlockSpec((1, ATTN_WIDTH), const2),
            pl.BlockSpec((1, ATTN_WIDTH), const2),
            pl.BlockSpec((ATTN_WIDTH, ATTN_WIDTH), const2),
            pl.BlockSpec((TM, LANES), ropeblk),
            pl.BlockSpec((TM, LANES), ropeblk),
        ],
        out_specs=[
            pl.BlockSpec((TM, ATTN_WIDTH), row),
            pl.BlockSpec((TM, ATTN_WIDTH), row),
            pl.BlockSpec((TM, ATTN_WIDTH), row),
            pl.BlockSpec((TM, CONV_WIDTH), row),
            pl.BlockSpec((TM, 2 * D_MODEL), row),
        ],
        out_shape=[
            jax.ShapeDtypeStruct((n_tok, ATTN_WIDTH), BF16),
            jax.ShapeDtypeStruct((n_tok, ATTN_WIDTH), F32),
            jax.ShapeDtypeStruct((n_tok, ATTN_WIDTH), F32),
            jax.ShapeDtypeStruct((n_tok, CONV_WIDTH), F32),
            jax.ShapeDtypeStruct((n_tok, 2 * D_MODEL), BF16),
        ],
        compiler_params=_cparams(1),
        name="inproj",
    )(x, mod_l, gmix, w_in_bf, qn, kn, gsum, cos_t, sin_t)


def _attn_kernel(*refs, lam_init, s_q, s_ctx, q_chunk):
    if s_ctx:
        q_ref, k_ref, v_ref, ck_ref, cv_ref, lam_ref, sub_ref, o_ref, kc_ref, vc_ref = refs
        kc_ref[0:s_ctx, :] = ck_ref[...].astype(BF16)
        vc_ref[0:s_ctx, :] = cv_ref[...].astype(BF16)
    else:
        q_ref, k_ref, v_ref, lam_ref, sub_ref, o_ref, kc_ref, vc_ref = refs
    kc_ref[s_ctx:s_ctx + s_q, :] = k_ref[...].astype(BF16)
    vc_ref[s_ctx:s_ctx + s_q, :] = v_ref[...].astype(BF16)
    lq = lam_ref[...]
    lam = (jnp.exp(jnp.sum(lq[0:1] * lq[1:2], axis=-1, keepdims=True))
           - jnp.exp(jnp.sum(lq[2:3] * lq[3:4], axis=-1, keepdims=True)) + lam_init)
    lane = lax.broadcasted_iota(jnp.int32, (q_chunk, LANES), 1)
    map0 = lane < HEAD_DIM
    sub = sub_ref[...]
    nt = (((1,), (1,)), ((), ()))

    def chunk(c, carry):
        r0 = pl.multiple_of(c * q_chunk, q_chunk)
        q = q_ref[pl.ds(r0, q_chunk), :]
        kc = kc_ref[...]
        zero = jnp.zeros_like(q)
        exps, inv_sums = [], []
        for mp in range(2):
            qm = jnp.where(map0, q, zero) if mp == 0 else jnp.where(map0, zero, q)
            s = lax.dot_general(qm, kc, nt, preferred_element_type=F32)
            e = jnp.exp(s - jnp.max(s, axis=-1, keepdims=True))
            inv_sums.append(1.0 / jnp.sum(e, axis=-1, keepdims=True))
            exps.append(e.astype(BF16))
        pv = jnp.dot(jnp.concatenate(exps, axis=0), vc_ref[...], preferred_element_type=F32)
        o = pv[0:q_chunk] * inv_sums[0] - lam * (pv[q_chunk:2 * q_chunk] * inv_sums[1])
        o = o * lax.rsqrt(jnp.mean(o * o, axis=-1, keepdims=True) + EPS) * sub * (1.0 - lam_init)
        o_ref[pl.ds(r0, q_chunk), :] = o.astype(BF16)
        return carry

    lax.fori_loop(0, s_q // q_chunk, chunk, 0, unroll=min(2, s_q // q_chunk))


def _attention(q, k, v, lam_qk_l, sub_l, lam_init, n_batch, s_q, row_blk0, ctx=None):
    s_ctx = 0 if ctx is None else ctx[0].shape[2]
    own = pl.BlockSpec((s_q, LANES), lambda b, h: (row_blk0 + b, h))
    in_specs = [own, own, own]
    args = [q, k, v]
    if ctx is not None:
        ck, cv, l = ctx
        cspec = pl.BlockSpec((None, None, s_ctx, LANES), lambda b, h: (b, l, 0, h))
        in_specs += [cspec, cspec]
        args += [ck, cv]
    in_specs += [pl.BlockSpec((4, HEAD_DIM), lambda b, h: (0, 0)),
                 pl.BlockSpec((1, LANES), lambda b, h: (0, 0))]
    args += [lam_qk_l, sub_l]
    return pl.pallas_call(
        functools.partial(_attn_kernel, lam_init=lam_init, s_q=s_q, s_ctx=s_ctx, q_chunk=min(s_q, 256)),
        grid=(n_batch, N_HEADS),
        in_specs=in_specs,
        out_specs=pl.BlockSpec((s_q, LANES), lambda b, h: (b, h)),
        out_shape=jax.ShapeDtypeStruct((n_batch * s_q, ATTN_WIDTH), BF16),
        scratch_shapes=[pltpu.VMEM((s_ctx + s_q, LANES), BF16), pltpu.VMEM((s_ctx + s_q, LANES), BF16)],
        compiler_params=_cparams(2),
        name="attn_ctx" if ctx is not None else "attn",
    )(*args)


def _conv_kernel(z_ref, w_ref, b_ref, g_ref, beta_ref, o_ref, pad_ref, *, s_len):
    halo = jnp.zeros((CONV_HALO, CONV_WIDTH), F32)
    pad_ref[0:CONV_HALO, :] = halo
    pad_ref[CONV_HALO:CONV_HALO + s_len, :] = z_ref[...]
    pad_ref[CONV_HALO + s_len:2 * CONV_HALO + s_len, :] = halo
    bias = b_ref[...]
    gam = g_ref[...]
    beta = beta_ref[...]
    off0 = CONV_HALO - CONV_KERNEL // 2

    def chunk(c, carry):
        r0 = pl.multiple_of(c * CONV_CHUNK, CONV_CHUNK)
        slabs = []
        for s0 in range(0, CONV_WIDTH, LANES):
            sl = slice(s0, s0 + LANES)
            win = pad_ref[pl.ds(r0, CONV_CHUNK + 2 * CONV_HALO), sl]
            part = jnp.broadcast_to(bias[:, sl], (CONV_CHUNK, LANES))
            for res in range(8):
                shifted = win[res:res + CONV_CHUNK + 2 * CONV_HALO - 8]
                for a8 in range(0, 2 * CONV_HALO, 8):
                    j = a8 + res - off0
                    if 0 <= j < CONV_KERNEL:
                        part = part + shifted[a8:a8 + CONV_CHUNK] * w_ref[j:j + 1, sl]
            slabs.append(part)
        acc = jnp.concatenate(slabs, axis=-1)
        mu = jnp.mean(acc, axis=-1, keepdims=True)
        d = acc - mu
        var = jnp.mean(d * d, axis=-1, keepdims=True)
        y = d * lax.rsqrt(var + EPS) * gam + beta
        o_ref[pl.ds(r0, CONV_CHUNK), :] = (y * jax.nn.sigmoid(y)).astype(BF16)
        return carry

    lax.fori_loop(0, s_len // CONV_CHUNK, chunk, 0)


def _conv_module(z, conv_w_l, conv_b_l, g_l, beta_l, n_batch, s_len, row_blk0):
    const2 = lambda b: (0, 0)
    return pl.pallas_call(
        functools.partial(_conv_kernel, s_len=s_len),
        grid=(n_batch,),
        in_specs=[
            pl.BlockSpec((s_len, CONV_WIDTH), lambda b: (row_blk0 + b, 0)),
            pl.BlockSpec((CONV_KERNEL, CONV_WIDTH), const2),
            pl.BlockSpec((1, CONV_WIDTH), const2),
            pl.BlockSpec((1, CONV_WIDTH), const2),
            pl.BlockSpec((1, CONV_WIDTH), const2),
        ],
        out_specs=pl.BlockSpec((s_len, CONV_WIDTH), lambda b: (b, 0)),
        out_shape=jax.ShapeDtypeStruct((n_batch * s_len, CONV_WIDTH), BF16),
        scratch_shapes=[pltpu.VMEM((s_len + 2 * CONV_HALO, CONV_WIDTH), F32)],
        compiler_params=_cparams(1),
        name="conv",
    )(z, conv_w_l, conv_b_l, g_l, beta_l)


def _route_rows(logits, seen, tri):
    rows = logits.shape[0]
    lane = lax.broadcasted_iota(jnp.int32, (rows, LANES), 1)
    lane_f = lane.astype(F32)
    work = logits
    sel_masks, top_vals, top_idx = [], [], []
    for _ in range(TOP_K):
        mx = jnp.max(work, axis=-1, keepdims=True)
        idx = jnp.min(jnp.where(work == mx, lane_f, float(LANES)), axis=-1, keepdims=True)
        sel = lane_f == idx
        sel_masks.append(sel)
        top_vals.append(mx)
        top_idx.append(idx)
        work = jnp.where(sel, NEG_BIG * 2.0, work)
    exps = [jnp.exp(v - top_vals[0]) for v in top_vals]
    denom = exps[0] + exps[1] + exps[2] + exps[3]
    onehot = jnp.zeros((rows, LANES), F32)
    for sel in sel_masks:
        onehot = onehot + jnp.where(sel, 1.0, 0.0)
    base = jnp.dot(tri, onehot.astype(BF16), preferred_element_type=F32) + seen
    te = jnp.zeros((rows, LANES), F32)
    gw = jnp.zeros((rows, LANES), F32)
    rk = jnp.zeros((rows, LANES), F32)
    for kk in range(TOP_K):
        r = jnp.sum(jnp.where(sel_masks[kk], base, 0.0), axis=-1, keepdims=True)
        col = lane == kk
        te = jnp.where(col, top_idx[kk], te)
        gw = jnp.where(col, exps[kk] / denom, gw)
        rk = jnp.where(col, r, rk)
    return te.astype(jnp.int32), gw, rk.astype(jnp.int32), jnp.sum(onehot, axis=0, keepdims=True)


def _outproj_kernel(x_ref, op_ref, os_ref, zp_ref, zs_ref, gate_ref, mod_ref, wa_ref, wc_ref, wo_ref,
                    gffn_ref, wrh_ref, wrl_ref, br_ref, tri_ref,
                    x1_ref, h2_ref, te_ref, gw_ref, rank_ref, cnt_ref, carry_ref, *, n_ctx_tiles):
    i = pl.program_id(0)

    @pl.when(i == 0)
    def _():
        carry_ref[...] = jnp.zeros_like(carry_ref)

    m = mod_ref[0]
    is_ctx = i < n_ctx_tiles
    seen = carry_ref[...]
    for g0 in range(0, TMO, TM):
        rs = slice(g0, g0 + TM)
        o = jnp.where(is_ctx, op_ref[rs, :], os_ref[rs, :])
        zc = jnp.where(is_ctx, zp_ref[rs, :], zs_ref[rs, :])
        attn_d = jnp.dot(o, wa_ref[...], preferred_element_type=F32)
        conv_d = jnp.dot(zc, wc_ref[...], preferred_element_type=F32)
        y = (gate_ref[rs, 0:D_MODEL].astype(F32) * attn_d
             + gate_ref[rs, D_MODEL:2 * D_MODEL].astype(F32) * conv_d)
        mix = jnp.dot(y.astype(BF16), wo_ref[...], preferred_element_type=F32)
        x1 = x_ref[rs, :] + m[2:3] * mix
        x1_ref[rs, :] = x1
        h2 = _rms_mod(x1, gffn_ref[...], m[3:4], m[4:5])
        h2_ref[rs, :] = h2
        h_hi = h2.astype(BF16)
        h_lo = (h2 - h_hi.astype(F32)).astype(BF16)
        logits = (jnp.dot(h_hi, wrh_ref[...], preferred_element_type=F32)
                  + (jnp.dot(h_lo, wrh_ref[...], preferred_element_type=F32)
                     + jnp.dot(h_hi, wrl_ref[...], preferred_element_type=F32))) + br_ref[...]
        te, gw, rk, cnt = _route_rows(logits, seen, tri_ref[...])
        te_ref[rs, :] = te
        gw_ref[rs, :] = gw
        rank_ref[rs, :] = rk
        seen = seen + cnt
    carry_ref[...] = seen
    cnt_ref[...] = jnp.broadcast_to(seen, cnt_ref.shape).astype(jnp.int32)


def _outproj(x, o_p, o_s, zc_p, zc_s, gates, mod_l, wa_bf, wc_bf, wo_bf, l, gffn, wr_hi, wr_lo, br_pad,
             tri, s_lat):
    n_tok = x.shape[0]
    n_ctx_tiles = o_p.shape[0] // TMO
    n_lat_tiles = o_s.shape[0] // TMO
    row = lambda i: (i, 0)
    ctxrow = lambda i: (jnp.minimum(i, n_ctx_tiles - 1), 0)
    latrow = lambda i: (jnp.clip(i - n_ctx_tiles, 0, n_lat_tiles - 1), 0)
    const2 = lambda i: (0, 0)
    lay3 = lambda i: (l, 0, 0)
    modrow = lambda i: (_mod_row_of_tile(i, n_ctx_tiles, s_lat // TMO), 0, 0)
    return pl.pallas_call(
        functools.partial(_outproj_kernel, n_ctx_tiles=n_ctx_tiles),
        grid=(n_tok // TMO,),
        in_specs=[
            pl.BlockSpec((TMO, D_MODEL), row),
            pl.BlockSpec((TMO, ATTN_WIDTH), ctxrow),
            pl.BlockSpec((TMO, ATTN_WIDTH), latrow),
            pl.BlockSpec((TMO, CONV_WIDTH), ctxrow),
            pl.BlockSpec((TMO, CONV_WIDTH), latrow),
            pl.BlockSpec((TMO, 2 * D_MODEL), row),
            pl.BlockSpec((1, 6, D_MODEL), modrow),
            pl.BlockSpec((None, ATTN_WIDTH, D_MODEL), lay3),
            pl.BlockSpec((None, CONV_WIDTH, D_MODEL), lay3),
            pl.BlockSpec((None, D_MODEL, D_MODEL), lay3),
            pl.BlockSpec((1, D_MODEL), const2),
            pl.BlockSpec((D_MODEL, LANES), const2),
            pl.BlockSpec((D_MODEL, LANES), const2),
            pl.BlockSpec((1, LANES), const2),
            pl.BlockSpec((TM, TM), const2),
        ],
        out_specs=[
            pl.BlockSpec((TMO, D_MODEL), row),
            pl.BlockSpec((TMO, D_MODEL), row),
            pl.BlockSpec((TMO, LANES), row),
            pl.BlockSpec((TMO, LANES), row),
            pl.BlockSpec((TMO, LANES), row),
            pl.BlockSpec((8, LANES), const2),
        ],
        out_shape=[
            jax.ShapeDtypeStruct((n_tok, D_MODEL), F32),
            jax.ShapeDtypeStruct((n_tok, D_MODEL), F32),
            jax.ShapeDtypeStruct((n_tok, LANES), jnp.int32),
            jax.ShapeDtypeStruct((n_tok, LANES), F32),
            jax.ShapeDtypeStruct((n_tok, LANES), jnp.int32),
            jax.ShapeDtypeStruct((8, LANES), jnp.int32),
        ],
        scratch_shapes=[pltpu.VMEM((1, LANES), F32)],
        compiler_params=_cparams(1),
        name="outproj_router",
    )(x, o_p, o_s, zc_p, zc_s, gates, mod_l, wa_bf, wc_bf, wo_bf, gffn, wr_hi, wr_lo, br_pad, tri)


def _invert_kernel(dest_ref, init_ref, inv_ref, sem):
    i = pl.program_id(0)

    @pl.when(i == 0)
    def _():
        cp = pltpu.make_async_copy(init_ref, inv_ref, sem)
        cp.start()
        cp.wait()

    base = i * INV_CHUNK

    def body(a, carry):
        inv_ref[dest_ref[a]] = base + a
        return carry

    lax.fori_loop(0, INV_CHUNK, body, 0, unroll=16)


def _invert(dest_flat, inv_init):
    return pl.pallas_call(
        _invert_kernel,
        grid=(dest_flat.shape[0] // INV_CHUNK,),
        in_specs=[
            pl.BlockSpec((INV_CHUNK,), lambda i: (i,), memory_space=pltpu.SMEM),
            pl.BlockSpec(memory_space=pl.ANY),
        ],
        out_specs=pl.BlockSpec(memory_space=pltpu.SMEM),
        out_shape=jax.ShapeDtypeStruct(inv_init.shape, jnp.int32),
        scratch_shapes=[pltpu.SemaphoreType.DMA(())],
        compiler_params=_cparams(1),
        name="moe_invert",
    )(dest_flat, inv_init)


def _expert_kernel(be_ref, nused_ref, inv_ref, h_ref, wgu_ref, bgu_ref, wd_ref, bd_ref, y_ref,
                   wgu_bf, wd_bf, xb0, xb1, xb2, gsem, *, n_blk):
    j = pl.program_id(0)
    n_used = nused_ref[0]
    xbuf = (xb0, xb1, xb2)

    def gather_copy(r, slot):
        tok = inv_ref[r] >> 2
        return pltpu.make_async_copy(h_ref.at[pl.ds(tok, 1), :], xbuf[slot].at[pl.ds(r, 1), :],
                                     gsem.at[slot])

    def wait_gather(slot):
        pltpu.make_async_copy(h_ref.at[pl.ds(0, EBLK), :], xbuf[slot], gsem.at[slot]).wait()

    def rolled_gather(slot):
        def body(r, carry):
            gather_copy(r, slot).start()
            return carry
        lax.fori_loop(0, EBLK, body, 0)

    @pl.when(j == 0)
    def _():
        rolled_gather(0)

    @pl.when(j == 1)
    def _():
        rolled_gather(1)

    blk = j - 2
    is_compute = jnp.logical_and(j >= 2, j <= n_used + 1)
    e_cur = be_ref[jnp.clip(blk, 0, n_blk - 1)]
    e_prev = be_ref[jnp.clip(blk - 1, 0, n_blk - 1)]

    @pl.when(jnp.logical_and(is_compute, jnp.logical_or(blk == 0, e_cur != e_prev)))
    def _():
        wgu_bf[...] = wgu_ref[...].astype(BF16)
        wd_bf[...] = wd_ref[...].astype(BF16)

    for slot in range(NBUF):
        nxt = (slot + 1) % NBUF
        prv = (slot + 2) % NBUF

        @pl.when(jnp.logical_and(is_compute, blk % NBUF == slot))
        def _():
            wait_gather(slot)
            for r in range(EBLK):
                gather_copy(r, prv).start()
            x = xbuf[slot][...].astype(BF16)
            gu = jnp.dot(x, wgu_bf[...], preferred_element_type=F32) + bgu_ref[...]
            g = jnp.minimum(gu[:, 0:D_EXPERT], SWIGLU_LIMIT)
            u = jnp.clip(gu[:, D_EXPERT:2 * D_EXPERT], -SWIGLU_LIMIT, SWIGLU_LIMIT)
            act = (u + 1.0) * (g * jax.nn.sigmoid(SWIGLU_ALPHA * g))
            y_ref[...] = jnp.dot(act.astype(BF16), wd_bf[...], preferred_element_type=F32) + bd_ref[...]

        @pl.when(jnp.logical_and(j == n_used + 2, (n_used - 1) % NBUF == slot))
        def _():
            wait_gather(nxt)
            wait_gather(prv)


def _experts(h2, inv, block_e, n_used, w_gate_up, b_gate_up, w_down, b_down, l):
    n_rows = inv.shape[0]
    n_blk = n_rows // EBLK
    cur = lambda j, nu: jnp.clip(j - 2, 0, nu[0] - 1)
    wblk = lambda j, be, nu: (l, be[cur(j, nu)], 0, 0)
    tile = pltpu.VMEM((EBLK, D_MODEL), F32)
    grid_spec = pltpu.PrefetchScalarGridSpec(
        num_scalar_prefetch=2,
        grid=(n_blk + 3,),
        in_specs=[
            pl.BlockSpec((EBLK,), lambda j, be, nu: (jnp.minimum(j, n_blk - 1),),
                         memory_space=pltpu.SMEM),
            pl.BlockSpec(memory_space=pl.ANY),
            pl.BlockSpec((None, None, D_MODEL, 2 * D_EXPERT), wblk),
            pl.BlockSpec((None, None, 1, 2 * D_EXPERT), wblk),
            pl.BlockSpec((None, None, D_EXPERT, D_MODEL), wblk),
            pl.BlockSpec((None, None, 1, D_MODEL), wblk),
        ],
        out_specs=pl.BlockSpec((EBLK, D_MODEL), lambda j, be, nu: (cur(j, nu), 0)),
        scratch_shapes=[
            pltpu.VMEM((D_MODEL, 2 * D_EXPERT), BF16),
            pltpu.VMEM((D_EXPERT, D_MODEL), BF16),
            tile, tile, tile,
            pltpu.SemaphoreType.DMA((NBUF,)),
        ],
    )
    return pl.pallas_call(
        functools.partial(_expert_kernel, n_blk=n_blk),
        grid_spec=grid_spec,
        out_shape=jax.ShapeDtypeStruct((n_rows, D_MODEL), F32),
        compiler_params=_cparams(1),
        name="moe_experts",
    )(block_e, n_used, inv, h2, w_gate_up,
      b_gate_up.reshape(DEPTH, N_EXPERTS, 1, 2 * D_EXPERT), w_down,
      b_down.reshape(DEPTH, N_EXPERTS, 1, D_MODEL))


def _combine_kernel(dcur_ref, dnxt_ref, x1_ref, mod_ref, gw_ref, yb_ref, o_ref, buf0, buf1, sem):
    i = pl.program_id(0)
    bufs = (buf0, buf1)

    def row_copy(dest_ref, r, kk, slot):
        return pltpu.make_async_copy(yb_ref.at[pl.ds(dest_ref[r * TOP_K + kk], 1), :],
                                     bufs[slot].at[kk, pl.ds(r, 1), :], sem.at[slot])

    def wait_tile(slot):
        for kk in range(TOP_K):
            pltpu.make_async_copy(yb_ref.at[pl.ds(0, TM), :], bufs[slot].at[kk], sem.at[slot]).wait()

    @pl.when(i == 0)
    def _():
        def body(r, carry):
            for kk in range(TOP_K):
                row_copy(dcur_ref, r, kk, 0).start()
            return carry
        lax.fori_loop(0, TM, body, 0)

    for slot in range(2):
        @pl.when(i % 2 == slot)
        def _():
            @pl.when(i + 1 < pl.num_programs(0))
            def _():
                for r in range(TM):
                    for kk in range(TOP_K):
                        row_copy(dnxt_ref, r, kk, 1 - slot).start()

            wait_tile(slot)
            gw = gw_ref[...]
            y = gw[:, 0:1] * bufs[slot][0]
            for kk in range(1, TOP_K):
                y = y + gw[:, kk:kk + 1] * bufs[slot][kk]
            o_ref[...] = x1_ref[...] + mod_ref[0][5:6] * y


def _combine(x1, mod_l, gatew, dest_flat, yb, n_ctx_tiles, tiles_per_seq):
    n_tok = x1.shape[0]
    n_tiles = n_tok // TM
    row = lambda i: (i, 0)
    modrow = lambda i: (_mod_row_of_tile(i, n_ctx_tiles, tiles_per_seq), 0, 0)
    gbuf = pltpu.VMEM((TOP_K, TM, D_MODEL), F32)
    return pl.pallas_call(
        _combine_kernel,
        grid=(n_tiles,),
        in_specs=[
            pl.BlockSpec((TM * TOP_K,), lambda i: (i,), memory_space=pltpu.SMEM),
            pl.BlockSpec((TM * TOP_K,), lambda i: (jnp.minimum(i + 1, n_tiles - 1),),
                         memory_space=pltpu.SMEM),
            pl.BlockSpec((TM, D_MODEL), row),
            pl.BlockSpec((1, 6, D_MODEL), modrow),
            pl.BlockSpec((TM, LANES), row),
            pl.BlockSpec(memory_space=pl.ANY),
        ],
        out_specs=pl.BlockSpec((TM, D_MODEL), row),
        out_shape=jax.ShapeDtypeStruct((n_tok, D_MODEL), F32),
        scratch_shapes=[gbuf, gbuf, pltpu.SemaphoreType.DMA((2,))],
        compiler_params=_cparams(1),
        name="moe_combine",
    )(dest_flat, dest_flat, x1, mod_l, gatew, yb)


def _rope_tables(s_len):
    rows = s_len // GRID_W
    row = jnp.repeat(jnp.arange(rows, dtype=F32), GRID_W)
    col = jnp.tile(jnp.arange(GRID_W, dtype=F32), rows)
    n_freq = HEAD_DIM // 4
    inv = ROPE_BASE ** (-jnp.arange(n_freq, dtype=F32) / n_freq)
    ang_r = row[:, None] * inv
    ang_c = col[:, None] * inv
    ang = jnp.concatenate([ang_r, ang_r, ang_c, ang_c], axis=1)
    sign = jnp.tile(jnp.concatenate([-jnp.ones((n_freq,), F32), jnp.ones((n_freq,), F32)]), 2)
    cos = jnp.tile(jnp.cos(ang), (1, 2))
    sin = jnp.tile(jnp.sin(ang) * sign, (1, 2))
    cos = jnp.concatenate([jnp.ones((TM, LANES), F32), cos], axis=0)
    sin = jnp.concatenate([jnp.zeros((TM, LANES), F32), sin], axis=0)
    return cos, sin


def kernel(x_prompt, x_sample, c, cache_k, cache_v, c_ctx, w_ada, b_ada, norm_mix, norm_ffn, w_in, q_norm, k_norm, lam_qk, sub_norm, w_attn_out, conv_w, conv_b, conv_norm_g, conv_norm_b, w_conv_out, w_out, w_router, b_router, w_gate_up, b_gate_up, w_down, b_down):
    b_p, s_p, _ = x_prompt.shape
    b_s, s_s, _ = x_sample.shape
    past = cache_k.shape[2]
    n_ctx = b_p * s_p
    n_tok = n_ctx + b_s * s_s
    n_ctx_tiles = n_ctx // TM
    tiles_per_seq = s_s // TM
    assert s_p == TM and s_s % TMO == 0 and n_ctx % s_s == 0 and s_s % GRID_W == 0
    assert (n_tok * TOP_K) % INV_CHUNK == 0 and (n_tok * TOP_K) % EBLK == 0

    x = jnp.concatenate([x_prompt.reshape(n_ctx, D_MODEL), x_sample.reshape(b_s * s_s, D_MODEL)], axis=0)
    n_cond = 8 * pl.cdiv(1 + b_s, 8)
    cond = jnp.zeros((n_cond, D_MODEL), F32).at[0].set(c_ctx).at[1:1 + b_s].set(c)
    mod = _ada_all_layers(cond, w_ada.astype(BF16), b_ada).reshape(DEPTH, n_cond, 6, D_MODEL)

    w_in_bf = w_in.astype(BF16)
    wa_bf = w_attn_out.astype(BF16)
    wc_bf = w_conv_out.astype(BF16)
    wo_bf = w_out.astype(BF16)
    ck = cache_k.reshape(b_s, DEPTH, past, ATTN_WIDTH)
    cv = cache_v.reshape(b_s, DEPTH, past, ATTN_WIDTH)
    cos_t, sin_t = _rope_tables(s_s)
    grp = jnp.arange(ATTN_WIDTH, dtype=jnp.int32) // HEAD_DIM
    gsum = (grp[:, None] == grp[None, :]).astype(BF16)
    tri = (jnp.arange(TM)[:, None] > jnp.arange(TM)[None, :]).astype(BF16)
    n_blk = (n_tok * TOP_K) // EBLK + N_EXPERTS
    n_rows = n_blk * EBLK
    blk_row0 = jnp.arange(n_blk, dtype=jnp.int32) * EBLK
    inv_init = jnp.zeros((n_rows,), jnp.int32)

    new_k, new_v = [], []
    for l in range(DEPTH):
        lam_init = 0.8 - 0.6 * math.exp(-0.3 * l)
        q, k, v, zpre, gates = _inproj(
            x, mod[l], norm_mix[l].reshape(1, D_MODEL), w_in_bf, l,
            jnp.tile(q_norm[l], ATTN_WIDTH // HEAD_DIM).reshape(1, ATTN_WIDTH),
            jnp.tile(k_norm[l], ATTN_WIDTH // HEAD_DIM).reshape(1, ATTN_WIDTH),
            gsum, cos_t, sin_t, n_ctx_tiles, tiles_per_seq)
        new_k.append(k[:n_ctx].reshape(b_p, s_p, N_HEADS, 2 * HEAD_DIM))
        new_v.append(v[:n_ctx].reshape(b_p, s_p, N_HEADS, 2 * HEAD_DIM))
        sub_l = sub_norm[l].reshape(1, LANES)
        o_p = _attention(q, k, v, lam_qk[l], sub_l, lam_init, b_p, s_p, 0)
        o_s = _attention(q, k, v, lam_qk[l], sub_l, lam_init, b_s, s_s, n_ctx // s_s, ctx=(ck, cv, l))
        conv_args = (conv_w[l], conv_b[l].reshape(1, CONV_WIDTH), conv_norm_g[l].reshape(1, CONV_WIDTH),
                     conv_norm_b[l].reshape(1, CONV_WIDTH))
        zc_p = _conv_module(zpre, *conv_args, b_p, s_p, 0)
        zc_s = _conv_module(zpre, *conv_args, b_s, s_s, n_ctx // s_s)
        wr_pad = jnp.zeros((D_MODEL, LANES), F32).at[:, :N_EXPERTS].set(w_router[l])
        wr_hi = wr_pad.astype(BF16)
        wr_lo = (wr_pad - wr_hi.astype(F32)).astype(BF16)
        br_pad = jnp.full((1, LANES), NEG_BIG, F32).at[0, :N_EXPERTS].set(b_router[l])
        x1, h2, top_e, gatew, rank, counts = _outproj(
            x, o_p, o_s, zc_p, zc_s, gates, mod[l], wa_bf, wc_bf, wo_bf, l,
            norm_ffn[l].reshape(1, D_MODEL), wr_hi, wr_lo, br_pad, tri, s_s)

        cnt = counts[0, :N_EXPERTS]
        padded = (cnt + EBLK - 1) // EBLK * EBLK
        pend = jnp.cumsum(padded)
        pstart = pend - padded
        block_e = jnp.minimum(jnp.sum((pend[None, :] <= blk_row0[:, None]).astype(jnp.int32), axis=1),
                              N_EXPERTS - 1)
        n_used = (pend[-1] // EBLK).astype(jnp.int32).reshape(1)

        dest = (pstart[top_e[:, :TOP_K]] + rank[:, :TOP_K]).reshape(n_tok * TOP_K)
        inv = _invert(dest, inv_init)
        yb = _experts(h2, inv, block_e, n_used, w_gate_up, b_gate_up, w_down, b_down, l)
        x = _combine(x1, mod[l], gatew, dest, yb, n_ctx_tiles, tiles_per_seq)

    y_prompt = x[:n_ctx].reshape(b_p, s_p, D_MODEL)
    y_sample = x[n_ctx:].reshape(b_s, s_s, D_MODEL)
    return (y_prompt, y_sample, jnp.stack(new_k, axis=1), jnp.stack(new_v, axis=1))
```

```python
import functools
import math

import jax
import jax.numpy as jnp
from jax import lax
from jax.experimental import pallas as pl
from jax.experimental.pallas import tpu as pltpu

F32 = jnp.float32
BF16 = jnp.bfloat16

D_MODEL = 1024
DEPTH = 4
GRID_W = 64
ATTN_WIDTH = 512
N_HEADS = 4
HEAD_DIM = 64
CONV_WIDTH = 512
CONV_KERNEL = 31
N_EXPERTS = 32
TOP_K = 4
D_EXPERT = 1024
SWIGLU_LIMIT = 7.0
SWIGLU_ALPHA = 1.702
ROPE_BASE = 10000.0
EPS = 1e-6

LANES = 128
TM = 256
TMO = 512
EBLK = 256
INV_CHUNK = 4096
NBUF = 3
ADA_TN = 1024
CONV_CHUNK = 64
CONV_HALO = 16
NEG_BIG = -1e30
VMEM_LIMIT = 56 * 1024 * 1024


def _cparams(n_axes, vmem=None):
    return pltpu.CompilerParams(
        dimension_semantics=("arbitrary",) * n_axes,
        vmem_limit_bytes=VMEM_LIMIT if vmem is None else vmem)


def _mod_row_of_tile(i, n_ctx_tiles, tiles_per_seq):
    return jnp.where(i < n_ctx_tiles, 0, 1 + (i - n_ctx_tiles) // tiles_per_seq)


def _ada_kernel(cond_ref, w_ref, b_ref, o_ref):
    cnd = cond_ref[...]
    act = cnd * jax.nn.sigmoid(cnd)
    o_ref[0] = jnp.dot(act.astype(BF16), w_ref[0], preferred_element_type=F32) + b_ref[0]


def _ada_all_layers(cond, w_ada_bf, b_ada):
    n_rows = cond.shape[0]
    n_col = w_ada_bf.shape[2]
    return pl.pallas_call(
        _ada_kernel,
        grid=(DEPTH, n_col // ADA_TN),
        in_specs=[
            pl.BlockSpec((n_rows, D_MODEL), lambda l, j: (0, 0)),
            pl.BlockSpec((1, D_MODEL, ADA_TN), lambda l, j: (l, 0, j)),
            pl.BlockSpec((1, 1, ADA_TN), lambda l, j: (l, 0, j)),
        ],
        out_specs=pl.BlockSpec((1, n_rows, ADA_TN), lambda l, j: (l, 0, j)),
        out_shape=jax.ShapeDtypeStruct((DEPTH, n_rows, n_col), F32),
        compiler_params=_cparams(2),
        name="ada",
    )(cond, w_ada_bf, b_ada.reshape(DEPTH, 1, n_col))


def _rms_mod(x, g, shift, scale):
    y = x * lax.rsqrt(jnp.mean(x * x, axis=-1, keepdims=True) + EPS) * g
    return y * (1.0 + scale) + shift


def _group_mean_sq(t, gsum):
    s = t * t
    hi = s.astype(BF16)
    lo = (s - hi.astype(F32)).astype(BF16)
    tot = jnp.dot(hi, gsum, preferred_element_type=F32) + jnp.dot(lo, gsum, preferred_element_type=F32)
    return tot * (1.0 / HEAD_DIM)


def _rope_slab(x, cos, sgn_sin, first_half):
    partner = jnp.where(first_half, pltpu.roll(x, LANES - 16, 1), pltpu.roll(x, 16, 1))
    return x * cos + partner * sgn_sin


def _inproj_kernel(*refs, combine_prev):
    if combine_prev:
        (dcur_ref, dnxt_ref, x1_ref, gw_ref, pmod_ref, yb_ref, mod_ref, gmix_ref, w_ref, qn_ref, kn_ref,
         gsum_ref, cos_ref, sin_ref, x_ref, q_ref, k_ref, v_ref, z_ref, gate_ref, buf0, buf1, sem) = refs
        i = pl.program_id(0)
        bufs = (buf0, buf1)

        def row_copy(dest_ref, r, kk, slot):
            return pltpu.make_async_copy(yb_ref.at[pl.ds(dest_ref[r * TOP_K + kk], 1), :],
                                         bufs[slot].at[kk, pl.ds(r, 1), :], sem.at[slot])

        def wait_tile(slot):
            for kk in range(TOP_K):
                pltpu.make_async_copy(yb_ref.at[pl.ds(0, TM), :], bufs[slot].at[kk], sem.at[slot]).wait()

        @pl.when(i == 0)
        def _():
            def body(r, carry):
                for kk in range(TOP_K):
                    row_copy(dcur_ref, r, kk, 0).start()
                return carry
            lax.fori_loop(0, TM, body, 0)

        for slot in range(2):
            @pl.when(i % 2 == slot)
            def _():
                wait_tile(slot)
                for r in range(TM):
                    for kk in range(TOP_K):
                        row_copy(dnxt_ref, r, kk, 1 - slot).start()
                gw = gw_ref[...]
                y = gw[:, 0:1] * bufs[slot][0]
                for kk in range(1, TOP_K):
                    y = y + gw[:, kk:kk + 1] * bufs[slot][kk]
                x = x1_ref[...] + pmod_ref[0][5:6] * y
                x_ref[...] = x
                _inproj_tile(x, mod_ref, gmix_ref, w_ref, qn_ref, kn_ref, gsum_ref, cos_ref, sin_ref,
                             q_ref, k_ref, v_ref, z_ref, gate_ref)

                @pl.when(i == pl.num_programs(0) - 1)
                def _():
                    wait_tile(1 - slot)
    else:
        (x_ref, mod_ref, gmix_ref, w_ref, qn_ref, kn_ref, gsum_ref, cos_ref, sin_ref,
         q_ref, k_ref, v_ref, z_ref, gate_ref) = refs
        _inproj_tile(x_ref[...], mod_ref, gmix_ref, w_ref, qn_ref, kn_ref, gsum_ref, cos_ref, sin_ref,
                     q_ref, k_ref, v_ref, z_ref, gate_ref)


def _inproj_tile(x, mod_ref, gmix_ref, w_ref, qn_ref, kn_ref, gsum_ref, cos_ref, sin_ref,
                 q_ref, k_ref, v_ref, z_ref, gate_ref):
    m = mod_ref[0]
    h = _rms_mod(x, gmix_ref[...], m[0:1], m[1:2]).astype(BF16)
    aw = ATTN_WIDTH
    q = jnp.dot(h, w_ref[:, 0:aw], preferred_element_type=F32)
    k = jnp.dot(h, w_ref[:, aw:2 * aw], preferred_element_type=F32)
    gsum = gsum_ref[...]
    q = q * lax.rsqrt(_group_mean_sq(q, gsum) + EPS) * qn_ref[...]
    k = k * lax.rsqrt(_group_mean_sq(k, gsum) + EPS) * kn_ref[...]
    cos = cos_ref[...]
    sin = sin_ref[...]
    lane = lax.broadcasted_iota(jnp.int32, (TM, LANES), 1)
    first_half = (lane & 16) == 0
    for hd in range(N_HEADS):
        sl = slice(hd * LANES, (hd + 1) * LANES)
        q_ref[:, sl] = (_rope_slab(q[:, sl], cos, sin, first_half) * (HEAD_DIM ** -0.5)).astype(BF16)
        k_ref[:, sl] = _rope_slab(k[:, sl], cos, sin, first_half)
    v_ref[...] = jnp.dot(h, w_ref[:, 2 * aw:3 * aw], preferred_element_type=F32)
    c0 = 3 * aw
    a = jnp.dot(h, w_ref[:, c0:c0 + CONV_WIDTH], preferred_element_type=F32)
    g = jnp.dot(h, w_ref[:, c0 + CONV_WIDTH:c0 + 2 * CONV_WIDTH], preferred_element_type=F32)
    z_ref[...] = a * jax.nn.sigmoid(g)
    c1 = c0 + 2 * CONV_WIDTH
    gl = jnp.dot(h, w_ref[:, c1:c1 + 2 * D_MODEL], preferred_element_type=F32)
    gate_ref[...] = jax.nn.sigmoid(gl).astype(BF16)


def _inproj(x, mod_l, gmix, w_in_bf, l, qn, kn, gsum, cos_t, sin_t, n_ctx_tiles, tiles_per_seq, prev=None):
    n_tok = (x if prev is None else prev[0]).shape[0]
    n_tiles = n_tok // TM
    in_w = w_in_bf.shape[2]
    row = lambda i: (i, 0)
    modrow = lambda i: (_mod_row_of_tile(i, n_ctx_tiles, tiles_per_seq), 0, 0)
    ropeblk = lambda i: (jnp.where(i < n_ctx_tiles, 0, 1 + (i - n_ctx_tiles) % tiles_per_seq), 0)
    const2 = lambda i: (0, 0)
    common_specs = [
        pl.BlockSpec((1, 6, D_MODEL), modrow),
        pl.BlockSpec((1, D_MODEL), const2),
        pl.BlockSpec((None, D_MODEL, in_w), lambda i: (l, 0, 0)),
        pl.BlockSpec((1, ATTN_WIDTH), const2),
        pl.BlockSpec((1, ATTN_WIDTH), const2),
        pl.BlockSpec((ATTN_WIDTH, ATTN_WIDTH), const2),
        pl.BlockSpec((TM, LANES), ropeblk),
        pl.BlockSpec((TM, LANES), ropeblk),
    ]
    common_args = (mod_l, gmix, w_in_bf, qn, kn, gsum, cos_t, sin_t)
    out_specs = [
        pl.BlockSpec((TM, ATTN_WIDTH), row),
        pl.BlockSpec((TM, ATTN_WIDTH), row),
        pl.BlockSpec((TM, ATTN_WIDTH), row),
        pl.BlockSpec((TM, CONV_WIDTH), row),
        pl.BlockSpec((TM, 2 * D_MODEL), row),
    ]
    out_shape = [
        jax.ShapeDtypeStruct((n_tok, ATTN_WIDTH), BF16),
        jax.ShapeDtypeStruct((n_tok, ATTN_WIDTH), F32),
        jax.ShapeDtypeStruct((n_tok, ATTN_WIDTH), F32),
        jax.ShapeDtypeStruct((n_tok, CONV_WIDTH), F32),
        jax.ShapeDtypeStruct((n_tok, 2 * D_MODEL), BF16),
    ]
    if prev is None:
        in_specs = [pl.BlockSpec((TM, D_MODEL), row)] + common_specs
        args = (x,) + common_args
        scratch = []
    else:
        x1, gatew, pmod, dest_flat, yb = prev
        in_specs = [
            pl.BlockSpec((TM * TOP_K,), lambda i: (i,), memory_space=pltpu.SMEM),
            pl.BlockSpec((TM * TOP_K,), lambda i: (jnp.minimum(i + 1, n_tiles - 1),),
                         memory_space=pltpu.SMEM),
            pl.BlockSpec((TM, D_MODEL), row),
            pl.BlockSpec((TM, LANES), row),
            pl.BlockSpec((1, 6, D_MODEL), modrow),
            pl.BlockSpec(memory_space=pl.ANY),
        ] + common_specs
        args = (dest_flat, dest_flat, x1, gatew, pmod, yb) + common_args
        out_specs = [pl.BlockSpec((TM, D_MODEL), row)] + out_specs
        out_shape = [jax.ShapeDtypeStruct((n_tok, D_MODEL), F32)] + out_shape
        gbuf = pltpu.VMEM((TOP_K, TM, D_MODEL), F32)
        scratch = [gbuf, gbuf, pltpu.SemaphoreType.DMA((2,))]
    return pl.pallas_call(
        functools.partial(_inproj_kernel, combine_prev=prev is not None),
        grid=(n_tiles,),
        in_specs=in_specs,
        out_specs=out_specs,
        out_shape=out_shape,
        scratch_shapes=scratch,
        compiler_params=_cparams(1),
        name="inproj" if prev is None else "combine_inproj",
    )(*args)


def _attn_kernel(*refs, lam_init, s_q, s_ctx, q_chunk):
    if s_ctx:
        q_ref, k_ref, v_ref, ck_ref, cv_ref, lam_ref, sub_ref, o_ref, kc_ref, vc_ref = refs
        kc_ref[0:s_ctx, :] = ck_ref[...].astype(BF16)
        vc_ref[0:s_ctx, :] = cv_ref[...].astype(BF16)
    else:
        q_ref, k_ref, v_ref, lam_ref, sub_ref, o_ref, kc_ref, vc_ref = refs
    kc_ref[s_ctx:s_ctx + s_q, :] = k_ref[...].astype(BF16)
    vc_ref[s_ctx:s_ctx + s_q, :] = v_ref[...].astype(BF16)
    lq = lam_ref[...]
    lam = (jnp.exp(jnp.sum(lq[0:1] * lq[1:2], axis=-1, keepdims=True))
           - jnp.exp(jnp.sum(lq[2:3] * lq[3:4], axis=-1, keepdims=True)) + lam_init)
    lane = lax.broadcasted_iota(jnp.int32, (q_chunk, LANES), 1)
    map0 = lane < HEAD_DIM
    sub = sub_ref[...]
    nt = (((1,), (1,)), ((), ()))

    def chunk(c, carry):
        r0 = pl.multiple_of(c * q_chunk, q_chunk)
        q = q_ref[pl.ds(r0, q_chunk), :]
        kc = kc_ref[...]
        zero = jnp.zeros_like(q)
        exps, inv_sums = [], []
        for mp in range(2):
            qm = jnp.where(map0, q, zero) if mp == 0 else jnp.where(map0, zero, q)
            s = lax.dot_general(qm, kc, nt, preferred_element_type=F32)
            e = jnp.exp(s - jnp.max(s, axis=-1, keepdims=True))
            inv_sums.append(1.0 / jnp.sum(e, axis=-1, keepdims=True))
            exps.append(e.astype(BF16))
        pv = jnp.dot(jnp.concatenate(exps, axis=0), vc_ref[...], preferred_element_type=F32)
        o = pv[0:q_chunk] * inv_sums[0] - lam * (pv[q_chunk:2 * q_chunk] * inv_sums[1])
        o = o * lax.rsqrt(jnp.mean(o * o, axis=-1, keepdims=True) + EPS) * sub * (1.0 - lam_init)
        o_ref[pl.ds(r0, q_chunk), :] = o.astype(BF16)
        return carry

    lax.fori_loop(0, s_q // q_chunk, chunk, 0, unroll=min(2, s_q // q_chunk))


def _attention(q, k, v, lam_qk_l, sub_l, lam_init, n_batch, s_q, row_blk0, ctx=None):
    s_ctx = 0 if ctx is None else ctx[0].shape[2]
    own = pl.BlockSpec((s_q, LANES), lambda b, h: (row_blk0 + b, h))
    in_specs = [own, own, own]
    args = [q, k, v]
    if ctx is not None:
        ck, cv, l = ctx
        cspec = pl.BlockSpec((None, None, s_ctx, LANES), lambda b, h: (b, l, 0, h))
        in_specs += [cspec, cspec]
        args += [ck, cv]
    in_specs += [pl.BlockSpec((4, HEAD_DIM), lambda b, h: (0, 0)),
                 pl.BlockSpec((1, LANES), lambda b, h: (0, 0))]
    args += [lam_qk_l, sub_l]
    return pl.pallas_call(
        functools.partial(_attn_kernel, lam_init=lam_init, s_q=s_q, s_ctx=s_ctx, q_chunk=min(s_q, 256)),
        grid=(n_batch, N_HEADS),
        in_specs=in_specs,
        out_specs=pl.BlockSpec((s_q, LANES), lambda b, h: (b, h)),
        out_shape=jax.ShapeDtypeStruct((n_batch * s_q, ATTN_WIDTH), BF16),
        scratch_shapes=[pltpu.VMEM((s_ctx + s_q, LANES), BF16), pltpu.VMEM((s_ctx + s_q, LANES), BF16)],
        compiler_params=_cparams(2),
        name="attn_ctx" if ctx is not None else "attn",
    )(*args)


def _conv_kernel(z_ref, w_ref, b_ref, g_ref, beta_ref, o_ref, pad_ref, *, s_len):
    zeros = jnp.zeros((2 * CONV_HALO - 8, CONV_WIDTH), F32)
    z = z_ref[...]
    for res in range(8):
        pad_ref[res, 0:CONV_HALO, :] = zeros[0:CONV_HALO]
        pad_ref[res, s_len + 8:s_len + 2 * CONV_HALO, :] = zeros
    for res in range(8):
        pad_ref[res, CONV_HALO - res:CONV_HALO - res + s_len, :] = z
    bias = b_ref[...]
    gam = g_ref[...]
    beta = beta_ref[...]
    off0 = CONV_HALO - CONV_KERNEL // 2

    def chunk(c, carry):
        r0 = pl.multiple_of(c * CONV_CHUNK, CONV_CHUNK)
        slabs = []
        for s0 in range(0, CONV_WIDTH, LANES):
            sl = slice(s0, s0 + LANES)
            part = jnp.broadcast_to(bias[:, sl], (CONV_CHUNK, LANES))
            for j in range(CONV_KERNEL):
                res = (off0 + j) % 8
                a8 = off0 + j - res
                part = part + pad_ref[res, pl.ds(r0 + a8, CONV_CHUNK), sl] * w_ref[j:j + 1, sl]
            slabs.append(part)
        acc = jnp.concatenate(slabs, axis=-1)
        mu = jnp.mean(acc, axis=-1, keepdims=True)
        d = acc - mu
        var = jnp.mean(d * d, axis=-1, keepdims=True)
        y = d * lax.rsqrt(var + EPS) * gam + beta
        o_ref[pl.ds(r0, CONV_CHUNK), :] = (y * jax.nn.sigmoid(y)).astype(BF16)
        return carry

    lax.fori_loop(0, s_len // CONV_CHUNK, chunk, 0)


def _conv_module(z, conv_w_l, conv_b_l, g_l, beta_l, n_batch, s_len, row_blk0):
    const2 = lambda b: (0, 0)
    return pl.pallas_call(
        functools.partial(_conv_kernel, s_len=s_len),
        grid=(n_batch,),
        in_specs=[
            pl.BlockSpec((s_len, CONV_WIDTH), lambda b: (row_blk0 + b, 0)),
            pl.BlockSpec((CONV_KERNEL, CONV_WIDTH), const2),
            pl.BlockSpec((1, CONV_WIDTH), const2),
            pl.BlockSpec((1, CONV_WIDTH), const2),
            pl.BlockSpec((1, CONV_WIDTH), const2),
        ],
        out_specs=pl.BlockSpec((s_len, CONV_WIDTH), lambda b: (b, 0)),
        out_shape=jax.ShapeDtypeStruct((n_batch * s_len, CONV_WIDTH), BF16),
        scratch_shapes=[pltpu.VMEM((8, s_len + 2 * CONV_HALO, CONV_WIDTH), F32)],
        compiler_params=_cparams(1),
        name="conv",
    )(z, conv_w_l, conv_b_l, g_l, beta_l)


def _route_rows(logits, seen, tri):
    rows = logits.shape[0]
    lane = lax.broadcasted_iota(jnp.int32, (rows, LANES), 1)
    lane_f = lane.astype(F32)
    work = logits
    sel_masks, top_vals, top_idx = [], [], []
    for _ in range(TOP_K):
        mx = jnp.max(work, axis=-1, keepdims=True)
        idx = jnp.min(jnp.where(work == mx, lane_f, float(LANES)), axis=-1, keepdims=True)
        sel = lane_f == idx
        sel_masks.append(sel)
        top_vals.append(mx)
        top_idx.append(idx)
        work = jnp.where(sel, NEG_BIG * 2.0, work)
    exps = [jnp.exp(v - top_vals[0]) for v in top_vals]
    denom = exps[0] + exps[1] + exps[2] + exps[3]
    onehot = jnp.zeros((rows, LANES), F32)
    for sel in sel_masks:
        onehot = onehot + jnp.where(sel, 1.0, 0.0)
    base = jnp.dot(tri, onehot.astype(BF16), preferred_element_type=F32) + seen
    te = jnp.zeros((rows, LANES), F32)
    gw = jnp.zeros((rows, LANES), F32)
    rk = jnp.zeros((rows, LANES), F32)
    for kk in range(TOP_K):
        r = jnp.sum(jnp.where(sel_masks[kk], base, 0.0), axis=-1, keepdims=True)
        col = lane == kk
        te = jnp.where(col, top_idx[kk], te)
        gw = jnp.where(col, exps[kk] / denom, gw)
        rk = jnp.where(col, r, rk)
    return te.astype(jnp.int32), gw, rk.astype(jnp.int32), jnp.sum(onehot, axis=0, keepdims=True)


def _outproj_kernel(x_ref, op_ref, os_ref, zp_ref, zs_ref, gate_ref, mod_ref, wa_ref, wc_ref, wo_ref,
                    gffn_ref, wrh_ref, wrl_ref, br_ref, tri_ref,
                    x1_ref, h2_ref, te_ref, gw_ref, rank_ref, cnt_ref, carry_ref, *, n_ctx_tiles):
    i = pl.program_id(0)

    @pl.when(i == 0)
    def _():
        carry_ref[...] = jnp.zeros_like(carry_ref)

    m = mod_ref[0]
    is_ctx = i < n_ctx_tiles
    seen = carry_ref[...]
    for g0 in range(0, TMO, TM):
        rs = slice(g0, g0 + TM)
        o = jnp.where(is_ctx, op_ref[rs, :], os_ref[rs, :])
        zc = jnp.where(is_ctx, zp_ref[rs, :], zs_ref[rs, :])
        attn_d = jnp.dot(o, wa_ref[...], preferred_element_type=F32)
        conv_d = jnp.dot(zc, wc_ref[...], preferred_element_type=F32)
        y = (gate_ref[rs, 0:D_MODEL].astype(F32) * attn_d
             + gate_ref[rs, D_MODEL:2 * D_MODEL].astype(F32) * conv_d)
        mix = jnp.dot(y.astype(BF16), wo_ref[...], preferred_element_type=F32)
        x1 = x_ref[rs, :] + m[2:3] * mix
        x1_ref[rs, :] = x1
        h2 = _rms_mod(x1, gffn_ref[...], m[3:4], m[4:5])
        h2_ref[rs, :] = h2
        h_hi = h2.astype(BF16)
        h_lo = (h2 - h_hi.astype(F32)).astype(BF16)
        logits = (jnp.dot(h_hi, wrh_ref[...], preferred_element_type=F32)
                  + (jnp.dot(h_lo, wrh_ref[...], preferred_element_type=F32)
                     + jnp.dot(h_hi, wrl_ref[...], preferred_element_type=F32))) + br_ref[...]
        te, gw, rk, cnt = _route_rows(logits, seen, tri_ref[...])
        te_ref[rs, :] = te
        gw_ref[rs, :] = gw
        rank_ref[rs, :] = rk
        seen = seen + cnt
    carry_ref[...] = seen
    cnt_ref[...] = jnp.broadcast_to(seen, cnt_ref.shape).astype(jnp.int32)


def _outproj(x, o_p, o_s, zc_p, zc_s, gates, mod_l, wa_bf, wc_bf, wo_bf, l, gffn, wr_hi, wr_lo, br_pad,
             tri, s_lat):
    n_tok = x.shape[0]
    n_ctx_tiles = o_p.shape[0] // TMO
    n_lat_tiles = o_s.shape[0] // TMO
    row = lambda i: (i, 0)
    ctxrow = lambda i: (jnp.minimum(i, n_ctx_tiles - 1), 0)
    latrow = lambda i: (jnp.clip(i - n_ctx_tiles, 0, n_lat_tiles - 1), 0)
    const2 = lambda i: (0, 0)
    lay3 = lambda i: (l, 0, 0)
    modrow = lambda i: (_mod_row_of_tile(i, n_ctx_tiles, s_lat // TMO), 0, 0)
    return pl.pallas_call(
        functools.partial(_outproj_kernel, n_ctx_tiles=n_ctx_tiles),
        grid=(n_tok // TMO,),
        in_specs=[
            pl.BlockSpec((TMO, D_MODEL), row),
            pl.BlockSpec((TMO, ATTN_WIDTH), ctxrow),
            pl.BlockSpec((TMO, ATTN_WIDTH), latrow),
            pl.BlockSpec((TMO, CONV_WIDTH), ctxrow),
            pl.BlockSpec((TMO, CONV_WIDTH), latrow),
            pl.BlockSpec((TMO, 2 * D_MODEL), row),
            pl.BlockSpec((1, 6, D_MODEL), modrow),
            pl.BlockSpec((None, ATTN_WIDTH, D_MODEL), lay3),
            pl.BlockSpec((None, CONV_WIDTH, D_MODEL), lay3),
            pl.BlockSpec((None, D_MODEL, D_MODEL), lay3),
            pl.BlockSpec((1, D_MODEL), const2),
            pl.BlockSpec((D_MODEL, LANES), const2),
            pl.BlockSpec((D_MODEL, LANES), const2),
            pl.BlockSpec((1, LANES), const2),
            pl.BlockSpec((TM, TM), const2),
        ],
        out_specs=[
            pl.BlockSpec((TMO, D_MODEL), row),
            pl.BlockSpec((TMO, D_MODEL), row),
            pl.BlockSpec((TMO, LANES), row),
            pl.BlockSpec((TMO, LANES), row),
            pl.BlockSpec((TMO, LANES), row),
            pl.BlockSpec((8, LANES), const2),
        ],
        out_shape=[
            jax.ShapeDtypeStruct((n_tok, D_MODEL), F32),
            jax.ShapeDtypeStruct((n_tok, D_MODEL), F32),
            jax.ShapeDtypeStruct((n_tok, LANES), jnp.int32),
            jax.ShapeDtypeStruct((n_tok, LANES), F32),
            jax.ShapeDtypeStruct((n_tok, LANES), jnp.int32),
            jax.ShapeDtypeStruct((8, LANES), jnp.int32),
        ],
        scratch_shapes=[pltpu.VMEM((1, LANES), F32)],
        compiler_params=_cparams(1),
        name="outproj_router",
    )(x, o_p, o_s, zc_p, zc_s, gates, mod_l, wa_bf, wc_bf, wo_bf, gffn, wr_hi, wr_lo, br_pad, tri)


def _invert_kernel(dest_ref, init_ref, inv_ref, sem):
    i = pl.program_id(0)

    @pl.when(i == 0)
    def _():
        cp = pltpu.make_async_copy(init_ref, inv_ref, sem)
        cp.start()
        cp.wait()

    base = i * INV_CHUNK

    def body(a, carry):
        inv_ref[dest_ref[a]] = base + a
        return carry

    lax.fori_loop(0, INV_CHUNK, body, 0, unroll=16)


def _invert(dest_flat, inv_init):
    return pl.pallas_call(
        _invert_kernel,
        grid=(dest_flat.shape[0] // INV_CHUNK,),
        in_specs=[
            pl.BlockSpec((INV_CHUNK,), lambda i: (i,), memory_space=pltpu.SMEM),
            pl.BlockSpec(memory_space=pl.ANY),
        ],
        out_specs=pl.BlockSpec(memory_space=pltpu.SMEM),
        out_shape=jax.ShapeDtypeStruct(inv_init.shape, jnp.int32),
        scratch_shapes=[pltpu.SemaphoreType.DMA(())],
        compiler_params=_cparams(1),
        name="moe_invert",
    )(dest_flat, inv_init)


def _expert_kernel(be_ref, nused_ref, inv_ref, h_ref, wgu_ref, bgu_ref, wd_ref, bd_ref, y_ref,
                   wgu_bf, wd_bf, xb0, xb1, xb2, gsem, *, n_blk):
    j = pl.program_id(0)
    n_used = nused_ref[0]
    xbuf = (xb0, xb1, xb2)

    def gather_copy(r, slot):
        tok = inv_ref[r] >> 2
        return pltpu.make_async_copy(h_ref.at[pl.ds(tok, 1), :], xbuf[slot].at[pl.ds(r, 1), :],
                                     gsem.at[slot])

    def wait_gather(slot):
        pltpu.make_async_copy(h_ref.at[pl.ds(0, EBLK), :], xbuf[slot], gsem.at[slot]).wait()

    def rolled_gather(slot):
        def body(r, carry):
            gather_copy(r, slot).start()
            return carry
        lax.fori_loop(0, EBLK, body, 0)

    @pl.when(j == 0)
    def _():
        rolled_gather(0)

    @pl.when(j == 1)
    def _():
        rolled_gather(1)

    blk = j - 2
    is_compute = jnp.logical_and(j >= 2, j <= n_used + 1)
    e_cur = be_ref[jnp.clip(blk, 0, n_blk - 1)]
    e_prev = be_ref[jnp.clip(blk - 1, 0, n_blk - 1)]

    @pl.when(jnp.logical_and(is_compute, jnp.logical_or(blk == 0, e_cur != e_prev)))
    def _():
        wgu_bf[...] = wgu_ref[...].astype(BF16)
        wd_bf[...] = wd_ref[...].astype(BF16)

    for slot in range(NBUF):
        nxt = (slot + 1) % NBUF
        prv = (slot + 2) % NBUF

        @pl.when(jnp.logical_and(is_compute, blk % NBUF == slot))
        def _():
            wait_gather(slot)
            for r in range(EBLK):
                gather_copy(r, prv).start()
            x = xbuf[slot][...].astype(BF16)
            gu = jnp.dot(x, wgu_bf[...], preferred_element_type=F32) + bgu_ref[...]
            g = jnp.minimum(gu[:, 0:D_EXPERT], SWIGLU_LIMIT)
            u = jnp.clip(gu[:, D_EXPERT:2 * D_EXPERT], -SWIGLU_LIMIT, SWIGLU_LIMIT)
            act = (u + 1.0) * (g * jax.nn.sigmoid(SWIGLU_ALPHA * g))
            y_ref[...] = jnp.dot(act.astype(BF16), wd_bf[...], preferred_element_type=F32) + bd_ref[...]

        @pl.when(jnp.logical_and(j == n_used + 2, (n_used - 1) % NBUF == slot))
        def _():
            wait_gather(nxt)
            wait_gather(prv)


def _experts(h2, inv, block_e, n_used, w_gate_up, b_gate_up, w_down, b_down, l):
    n_rows = inv.shape[0]
    n_blk = n_rows // EBLK
    cur = lambda j, nu: jnp.clip(j - 2, 0, nu[0] - 1)
    wblk = lambda j, be, nu: (l, be[cur(j, nu)], 0, 0)
    tile = pltpu.VMEM((EBLK, D_MODEL), F32)
    grid_spec = pltpu.PrefetchScalarGridSpec(
        num_scalar_prefetch=2,
        grid=(n_blk + 3,),
        in_specs=[
            pl.BlockSpec((EBLK,), lambda j, be, nu: (jnp.minimum(j, n_blk - 1),),
                         memory_space=pltpu.SMEM),
            pl.BlockSpec(memory_space=pl.ANY),
            pl.BlockSpec((None, None, D_MODEL, 2 * D_EXPERT), wblk),
            pl.BlockSpec((None, None, 1, 2 * D_EXPERT), wblk),
            pl.BlockSpec((None, None, D_EXPERT, D_MODEL), wblk),
            pl.BlockSpec((None, None, 1, D_MODEL), wblk),
        ],
        out_specs=pl.BlockSpec((EBLK, D_MODEL), lambda j, be, nu: (cur(j, nu), 0)),
        scratch_shapes=[
            pltpu.VMEM((D_MODEL, 2 * D_EXPERT), BF16),
            pltpu.VMEM((D_EXPERT, D_MODEL), BF16),
            tile, tile, tile,
            pltpu.SemaphoreType.DMA((NBUF,)),
        ],
    )
    return pl.pallas_call(
        functools.partial(_expert_kernel, n_blk=n_blk),
        grid_spec=grid_spec,
        out_shape=jax.ShapeDtypeStruct((n_rows, D_MODEL), F32),
        compiler_params=_cparams(1),
        name="moe_experts",
    )(block_e, n_used, inv, h2, w_gate_up,
      b_gate_up.reshape(DEPTH, N_EXPERTS, 1, 2 * D_EXPERT), w_down,
      b_down.reshape(DEPTH, N_EXPERTS, 1, D_MODEL))


def _combine_kernel(dcur_ref, dnxt_ref, x1_ref, mod_ref, gw_ref, yb_ref, o_ref, buf0, buf1, sem):
    i = pl.program_id(0)
    bufs = (buf0, buf1)

    def row_copy(dest_ref, r, kk, slot):
        return pltpu.make_async_copy(yb_ref.at[pl.ds(dest_ref[r * TOP_K + kk], 1), :],
                                     bufs[slot].at[kk, pl.ds(r, 1), :], sem.at[slot])

    def wait_tile(slot):
        for kk in range(TOP_K):
            pltpu.make_async_copy(yb_ref.at[pl.ds(0, TM), :], bufs[slot].at[kk], sem.at[slot]).wait()

    @pl.when(i == 0)
    def _():
        def body(r, carry):
            for kk in range(TOP_K):
                row_copy(dcur_ref, r, kk, 0).start()
            return carry
        lax.fori_loop(0, TM, body, 0)

    for slot in range(2):
        @pl.when(i % 2 == slot)
        def _():
            @pl.when(i + 1 < pl.num_programs(0))
            def _():
                for r in range(TM):
                    for kk in range(TOP_K):
                        row_copy(dnxt_ref, r, kk, 1 - slot).start()

            wait_tile(slot)
            gw = gw_ref[...]
            y = gw[:, 0:1] * bufs[slot][0]
            for kk in range(1, TOP_K):
                y = y + gw[:, kk:kk + 1] * bufs[slot][kk]
            o_ref[...] = x1_ref[...] + mod_ref[0][5:6] * y


def _combine(x1, mod_l, gatew, dest_flat, yb, n_ctx_tiles, tiles_per_seq):
    n_tok = x1.shape[0]
    n_tiles = n_tok // TM
    row = lambda i: (i, 0)
    modrow = lambda i: (_mod_row_of_tile(i, n_ctx_tiles, tiles_per_seq), 0, 0)
    gbuf = pltpu.VMEM((TOP_K, TM, D_MODEL), F32)
    return pl.pallas_call(
        _combine_kernel,
        grid=(n_tiles,),
        in_specs=[
            pl.BlockSpec((TM * TOP_K,), lambda i: (i,), memory_space=pltpu.SMEM),
            pl.BlockSpec((TM * TOP_K,), lambda i: (jnp.minimum(i + 1, n_tiles - 1),),
                         memory_space=pltpu.SMEM),
            pl.BlockSpec((TM, D_MODEL), row),
            pl.BlockSpec((1, 6, D_MODEL), modrow),
            pl.BlockSpec((TM, LANES), row),
            pl.BlockSpec(memory_space=pl.ANY),
        ],
        out_specs=pl.BlockSpec((TM, D_MODEL), row),
        out_shape=jax.ShapeDtypeStruct((n_tok, D_MODEL), F32),
        scratch_shapes=[gbuf, gbuf, pltpu.SemaphoreType.DMA((2,))],
        compiler_params=_cparams(1),
        name="moe_combine",
    )(dest_flat, dest_flat, x1, mod_l, gatew, yb)


def _rope_tables(s_len):
    rows = s_len // GRID_W
    row = jnp.repeat(jnp.arange(rows, dtype=F32), GRID_W)
    col = jnp.tile(jnp.arange(GRID_W, dtype=F32), rows)
    n_freq = HEAD_DIM // 4
    inv = ROPE_BASE ** (-jnp.arange(n_freq, dtype=F32) / n_freq)
    ang_r = row[:, None] * inv
    ang_c = col[:, None] * inv
    ang = jnp.concatenate([ang_r, ang_r, ang_c, ang_c], axis=1)
    sign = jnp.tile(jnp.concatenate([-jnp.ones((n_freq,), F32), jnp.ones((n_freq,), F32)]), 2)
    cos = jnp.tile(jnp.cos(ang), (1, 2))
    sin = jnp.tile(jnp.sin(ang) * sign, (1, 2))
    cos = jnp.concatenate([jnp.ones((TM, LANES), F32), cos], axis=0)
    sin = jnp.concatenate([jnp.zeros((TM, LANES), F32), sin], axis=0)
    return cos, sin


def kernel(x_prompt, x_sample, c, cache_k, cache_v, c_ctx, w_ada, b_ada, norm_mix, norm_ffn, w_in, q_norm, k_norm, lam_qk, sub_norm, w_attn_out, conv_w, conv_b, conv_norm_g, conv_norm_b, w_conv_out, w_out, w_router, b_router, w_gate_up, b_gate_up, w_down, b_down):
    b_p, s_p, _ = x_prompt.shape
    b_s, s_s, _ = x_sample.shape
    past = cache_k.shape[2]
    n_ctx = b_p * s_p
    n_tok = n_ctx + b_s * s_s
    n_ctx_tiles = n_ctx // TM
    tiles_per_seq = s_s // TM
    assert s_p == TM and s_s % TMO == 0 and n_ctx % s_s == 0 and s_s % GRID_W == 0
    assert (n_tok * TOP_K) % INV_CHUNK == 0 and (n_tok * TOP_K) % EBLK == 0

    x = jnp.concatenate([x_prompt.reshape(n_ctx, D_MODEL), x_sample.reshape(b_s * s_s, D_MODEL)], axis=0)
    n_cond = 8 * pl.cdiv(1 + b_s, 8)
    cond = jnp.zeros((n_cond, D_MODEL), F32).at[0].set(c_ctx).at[1:1 + b_s].set(c)
    mod = _ada_all_layers(cond, w_ada.astype(BF16), b_ada).reshape(DEPTH, n_cond, 6, D_MODEL)

    w_in_bf = w_in.astype(BF16)
    wa_bf = w_attn_out.astype(BF16)
    wc_bf = w_conv_out.astype(BF16)
    wo_bf = w_out.astype(BF16)
    ck = cache_k.reshape(b_s, DEPTH, past, ATTN_WIDTH)
    cv = cache_v.reshape(b_s, DEPTH, past, ATTN_WIDTH)
    cos_t, sin_t = _rope_tables(s_s)
    grp = jnp.arange(ATTN_WIDTH, dtype=jnp.int32) // HEAD_DIM
    gsum = (grp[:, None] == grp[None, :]).astype(BF16)
    tri = (jnp.arange(TM)[:, None] > jnp.arange(TM)[None, :]).astype(BF16)
    n_blk = (n_tok * TOP_K) // EBLK + N_EXPERTS
    n_rows = n_blk * EBLK
    blk_row0 = jnp.arange(n_blk, dtype=jnp.int32) * EBLK
    inv_init = jnp.zeros((n_rows,), jnp.int32)

    new_k, new_v = [], []
    pending = None
    for l in range(DEPTH):
        lam_init = 0.8 - 0.6 * math.exp(-0.3 * l)
        proj = _inproj(
            x, mod[l], norm_mix[l].reshape(1, D_MODEL), w_in_bf, l,
            jnp.tile(q_norm[l], ATTN_WIDTH // HEAD_DIM).reshape(1, ATTN_WIDTH),
            jnp.tile(k_norm[l], ATTN_WIDTH // HEAD_DIM).reshape(1, ATTN_WIDTH),
            gsum, cos_t, sin_t, n_ctx_tiles, tiles_per_seq, prev=pending)
        if pending is not None:
            x, proj = proj[0], proj[1:]
        q, k, v, zpre, gates = proj
        new_k.append(k[:n_ctx].reshape(b_p, s_p, N_HEADS, 2 * HEAD_DIM))
        new_v.append(v[:n_ctx].reshape(b_p, s_p, N_HEADS, 2 * HEAD_DIM))
        sub_l = sub_norm[l].reshape(1, LANES)
        o_p = _attention(q, k, v, lam_qk[l], sub_l, lam_init, b_p, s_p, 0)
        o_s = _attention(q, k, v, lam_qk[l], sub_l, lam_init, b_s, s_s, n_ctx // s_s, ctx=(ck, cv, l))
        conv_args = (conv_w[l], conv_b[l].reshape(1, CONV_WIDTH), conv_norm_g[l].reshape(1, CONV_WIDTH),
                     conv_norm_b[l].reshape(1, CONV_WIDTH))
        zc_p = _conv_module(zpre, *conv_args, b_p, s_p, 0)
        zc_s = _conv_module(zpre, *conv_args, b_s, s_s, n_ctx // s_s)
        wr_pad = jnp.zeros((D_MODEL, LANES), F32).at[:, :N_EXPERTS].set(w_router[l])
        wr_hi = wr_pad.astype(BF16)
        wr_lo = (wr_pad - wr_hi.astype(F32)).astype(BF16)
        br_pad = jnp.full((1, LANES), NEG_BIG, F32).at[0, :N_EXPERTS].set(b_router[l])
        x1, h2, top_e, gatew, rank, counts = _outproj(
            x, o_p, o_s, zc_p, zc_s, gates, mod[l], wa_bf, wc_bf, wo_bf, l,
            norm_ffn[l].reshape(1, D_MODEL), wr_hi, wr_lo, br_pad, tri, s_s)

        cnt = counts[0, :N_EXPERTS]
        padded = (cnt + EBLK - 1) // EBLK * EBLK
        pend = jnp.cumsum(padded)
        pstart = pend - padded
        block_e = jnp.minimum(jnp.sum((pend[None, :] <= blk_row0[:, None]).astype(jnp.int32), axis=1),
                              N_EXPERTS - 1)
        n_used = (pend[-1] // EBLK).astype(jnp.int32).reshape(1)

        dest = (pstart[top_e[:, :TOP_K]] + rank[:, :TOP_K]).reshape(n_tok * TOP_K)
        inv = _invert(dest, inv_init)
        yb = _experts(h2, inv, block_e, n_used, w_gate_up, b_gate_up, w_down, b_down, l)
        pending = (x1, gatew, mod[l], dest, yb)

    x1, gatew, mod_last, dest, yb = pending
    x = _combine(x1, mod_last, gatew, dest, yb, n_ctx_tiles, tiles_per_seq)

    y_prompt = x[:n_ctx].reshape(b_p, s_p, D_MODEL)
    y_sample = x[n_ctx:].reshape(b_s, s_s, D_MODEL)
    return (y_prompt, y_sample, jnp.stack(new_k, axis=1), jnp.stack(new_v, axis=1))
```

```python
import functools
import math

import jax
import jax.numpy as jnp
from jax import lax
from jax.experimental import pallas as pl
from jax.experimental.pallas import tpu as pltpu

F32 = jnp.float32
BF16 = jnp.bfloat16

D_MODEL = 1024
DEPTH = 4
GRID_W = 64
ATTN_WIDTH = 512
N_HEADS = 4
HEAD_DIM = 64
CONV_WIDTH = 512
CONV_KERNEL = 31
N_EXPERTS = 32
TOP_K = 4
D_EXPERT = 1024
SWIGLU_LIMIT = 7.0
SWIGLU_ALPHA = 1.702
ROPE_BASE = 10000.0
EPS = 1e-6

LANES = 128
TM = 256
TMO = 512
EBLK = 256
INV_CHUNK = 4096
NBUF = 3
ADA_TN = 1024
CONV_CHUNK = 64
CONV_HALO = 16
NEG_BIG = -1e30
VMEM_LIMIT = 56 * 1024 * 1024


def _cparams(n_axes, vmem=None):
    return pltpu.CompilerParams(
        dimension_semantics=("arbitrary",) * n_axes,
        vmem_limit_bytes=VMEM_LIMIT if vmem is None else vmem)


def _mod_row_of_tile(i, n_ctx_tiles, tiles_per_seq):
    return jnp.where(i < n_ctx_tiles, 0, 1 + (i - n_ctx_tiles) // tiles_per_seq)


def _ada_kernel(cond_ref, w_ref, b_ref, o_ref):
    cnd = cond_ref[...]
    act = cnd * jax.nn.sigmoid(cnd)
    o_ref[0] = jnp.dot(act.astype(BF16), w_ref[0], preferred_element_type=F32) + b_ref[0]


def _ada_all_layers(cond, w_ada_bf, b_ada):
    n_rows = cond.shape[0]
    n_col = w_ada_bf.shape[2]
    return pl.pallas_call(
        _ada_kernel,
        grid=(DEPTH, n_col // ADA_TN),
        in_specs=[
            pl.BlockSpec((n_rows, D_MODEL), lambda l, j: (0, 0)),
            pl.BlockSpec((1, D_MODEL, ADA_TN), lambda l, j: (l, 0, j)),
            pl.BlockSpec((1, 1, ADA_TN), lambda l, j: (l, 0, j)),
        ],
        out_specs=pl.BlockSpec((1, n_rows, ADA_TN), lambda l, j: (l, 0, j)),
        out_shape=jax.ShapeDtypeStruct((DEPTH, n_rows, n_col), F32),
        compiler_params=_cparams(2),
        name="ada",
    )(cond, w_ada_bf, b_ada.reshape(DEPTH, 1, n_col))


def _rms_mod(x, g, shift, scale):
    y = x * lax.rsqrt(jnp.mean(x * x, axis=-1, keepdims=True) + EPS) * g
    return y * (1.0 + scale) + shift


def _group_mean_sq(t, gsum):
    s = t * t
    hi = s.astype(BF16)
    lo = (s - hi.astype(F32)).astype(BF16)
    tot = jnp.dot(hi, gsum, preferred_element_type=F32) + jnp.dot(lo, gsum, preferred_element_type=F32)
    return tot * (1.0 / HEAD_DIM)


def _rope_slab(x, cos, sgn_sin, first_half):
    partner = jnp.where(first_half, pltpu.roll(x, LANES - 16, 1), pltpu.roll(x, 16, 1))
    return x * cos + partner * sgn_sin


def _inproj_kernel(x_ref, mod_ref, gmix_ref, w_ref, qn_ref, kn_ref, gsum_ref, cos_ref, sin_ref,
                   q_ref, k_ref, v_ref, z_ref, gate_ref):
    m = mod_ref[0]
    h = _rms_mod(x_ref[...], gmix_ref[...], m[0:1], m[1:2]).astype(BF16)
    aw = ATTN_WIDTH
    q = jnp.dot(h, w_ref[:, 0:aw], preferred_element_type=F32)
    k = jnp.dot(h, w_ref[:, aw:2 * aw], preferred_element_type=F32)
    gsum = gsum_ref[...]
    q = q * lax.rsqrt(_group_mean_sq(q, gsum) + EPS) * qn_ref[...]
    k = k * lax.rsqrt(_group_mean_sq(k, gsum) + EPS) * kn_ref[...]
    cos = cos_ref[...]
    sin = sin_ref[...]
    lane = lax.broadcasted_iota(jnp.int32, (TM, LANES), 1)
    first_half = (lane & 16) == 0
    for hd in range(N_HEADS):
        sl = slice(hd * LANES, (hd + 1) * LANES)
        q_ref[:, sl] = (_rope_slab(q[:, sl], cos, sin, first_half) * (HEAD_DIM ** -0.5)).astype(BF16)
        k_ref[:, sl] = _rope_slab(k[:, sl], cos, sin, first_half)
    v_ref[...] = jnp.dot(h, w_ref[:, 2 * aw:3 * aw], preferred_element_type=F32)
    c0 = 3 * aw
    a = jnp.dot(h, w_ref[:, c0:c0 + CONV_WIDTH], preferred_element_type=F32)
    g = jnp.dot(h, w_ref[:, c0 + CONV_WIDTH:c0 + 2 * CONV_WIDTH], preferred_element_type=F32)
    z_ref[...] = a * jax.nn.sigmoid(g)
    c1 = c0 + 2 * CONV_WIDTH
    gl = jnp.dot(h, w_ref[:, c1:c1 + 2 * D_MODEL], preferred_element_type=F32)
    gate_ref[...] = jax.nn.sigmoid(gl).astype(BF16)


def _inproj(x, mod_l, gmix, w_in_bf, l, qn, kn, gsum, cos_t, sin_t, n_ctx_tiles, tiles_per_seq):
    n_tok = x.shape[0]
    in_w = w_in_bf.shape[2]
    row = lambda i: (i, 0)
    modrow = lambda i: (_mod_row_of_tile(i, n_ctx_tiles, tiles_per_seq), 0, 0)
    ropeblk = lambda i: (jnp.where(i < n_ctx_tiles, 0, 1 + (i - n_ctx_tiles) % tiles_per_seq), 0)
    const2 = lambda i: (0, 0)
    return pl.pallas_call(
        _inproj_kernel,
        grid=(n_tok // TM,),
        in_specs=[
            pl.BlockSpec((TM, D_MODEL), row),
            pl.BlockSpec((1, 6, D_MODEL), modrow),
            pl.BlockSpec((1, D_MODEL), const2),
            pl.BlockSpec((None, D_MODEL, in_w), lambda i: (l, 0, 0)),
            pl.BlockSpec((1, ATTN_WIDTH), const2),
            pl.BlockSpec((1, ATTN_WIDTH), const2),
            pl.BlockSpec((ATTN_WIDTH, ATTN_WIDTH), const2),
            pl.BlockSpec((TM, LANES), ropeblk),
            pl.BlockSpec((TM, LANES), ropeblk),
        ],
        out_specs=[
            pl.BlockSpec((TM, ATTN_WIDTH), row),
            pl.BlockSpec((TM, ATTN_WIDTH), row),
            pl.BlockSpec((TM, ATTN_WIDTH), row),
            pl.BlockSpec((TM, CONV_WIDTH), row),
            pl.BlockSpec((TM, 2 * D_MODEL), row),
        ],
        out_shape=[
            jax.ShapeDtypeStruct((n_tok, ATTN_WIDTH), BF16),
            jax.ShapeDtypeStruct((n_tok, ATTN_WIDTH), F32),
            jax.ShapeDtypeStruct((n_tok, ATTN_WIDTH), F32),
            jax.ShapeDtypeStruct((n_tok, CONV_WIDTH), F32),
            jax.ShapeDtypeStruct((n_tok, 2 * D_MODEL), BF16),
        ],
        compiler_params=_cparams(1),
        name="inproj",
    )(x, mod_l, gmix, w_in_bf, qn, kn, gsum, cos_t, sin_t)


def _attn_kernel(*refs, lam_init, s_q, s_ctx, q_chunk):
    if s_ctx:
        q_ref, k_ref, v_ref, ck_ref, cv_ref, lam_ref, sub_ref, o_ref, kc_ref, vc_ref = refs
        kc_ref[0:s_ctx, :] = ck_ref[...].astype(BF16)
        vc_ref[0:s_ctx, :] = cv_ref[...].astype(BF16)
    else:
        q_ref, k_ref, v_ref, lam_ref, sub_ref, o_ref, kc_ref, vc_ref = refs
    kc_ref[s_ctx:s_ctx + s_q, :] = k_ref[...].astype(BF16)
    vc_ref[s_ctx:s_ctx + s_q, :] = v_ref[...].astype(BF16)
    lq = lam_ref[...]
    lam = (jnp.exp(jnp.sum(lq[0:1] * lq[1:2], axis=-1, keepdims=True))
           - jnp.exp(jnp.sum(lq[2:3] * lq[3:4], axis=-1, keepdims=True)) + lam_init)
    lane = lax.broadcasted_iota(jnp.int32, (q_chunk, LANES), 1)
    map0 = lane < HEAD_DIM
    sub = sub_ref[...]
    nt = (((1,), (1,)), ((), ()))

    def chunk(c, carry):
        r0 = pl.multiple_of(c * q_chunk, q_chunk)
        q = q_ref[pl.ds(r0, q_chunk), :]
        kc = kc_ref[...]
        zero = jnp.zeros_like(q)
        exps, inv_sums = [], []
        for mp in range(2):
            qm = jnp.where(map0, q, zero) if mp == 0 else jnp.where(map0, zero, q)
            s = lax.dot_general(qm, kc, nt, preferred_element_type=F32)
            e = jnp.exp(s - jnp.max(s, axis=-1, keepdims=True))
            inv_sums.append(1.0 / jnp.sum(e, axis=-1, keepdims=True))
            exps.append(e.astype(BF16))
        pv = jnp.dot(jnp.concatenate(exps, axis=0), vc_ref[...], preferred_element_type=F32)
        o = pv[0:q_chunk] * inv_sums[0] - lam * (pv[q_chunk:2 * q_chunk] * inv_sums[1])
        o = o * lax.rsqrt(jnp.mean(o * o, axis=-1, keepdims=True) + EPS) * sub * (1.0 - lam_init)
        o_ref[pl.ds(r0, q_chunk), :] = o.astype(BF16)
        return carry

    lax.fori_loop(0, s_q // q_chunk, chunk, 0, unroll=min(2, s_q // q_chunk))


def _attention(q, k, v, lam_qk_l, sub_l, lam_init, n_batch, s_q, row_blk0, ctx=None):
    s_ctx = 0 if ctx is None else ctx[0].shape[2]
    own = pl.BlockSpec((s_q, LANES), lambda b, h: (row_blk0 + b, h))
    in_specs = [own, own, own]
    args = [q, k, v]
    if ctx is not None:
        ck, cv, l = ctx
        cspec = pl.BlockSpec((None, None, s_ctx, LANES), lambda b, h: (b, l, 0, h))
        in_specs += [cspec, cspec]
        args += [ck, cv]
    in_specs += [pl.BlockSpec((4, HEAD_DIM), lambda b, h: (0, 0)),
                 pl.BlockSpec((1, LANES), lambda b, h: (0, 0))]
    args += [lam_qk_l, sub_l]
    return pl.pallas_call(
        functools.partial(_attn_kernel, lam_init=lam_init, s_q=s_q, s_ctx=s_ctx, q_chunk=min(s_q, 256)),
        grid=(n_batch, N_HEADS),
        in_specs=in_specs,
        out_specs=pl.BlockSpec((s_q, LANES), lambda b, h: (b, h)),
        out_shape=jax.ShapeDtypeStruct((n_batch * s_q, ATTN_WIDTH), BF16),
        scratch_shapes=[pltpu.VMEM((s_ctx + s_q, LANES), BF16), pltpu.VMEM((s_ctx + s_q, LANES), BF16)],
        compiler_params=_cparams(2),
        name="attn_ctx" if ctx is not None else "attn",
    )(*args)


def _conv_kernel(z_ref, w_ref, b_ref, g_ref, beta_ref, o_ref, pad_ref, *, s_len):
    zeros = jnp.zeros((2 * CONV_HALO - 8, CONV_WIDTH), F32)
    z = z_ref[...]
    for res in range(8):
        pad_ref[res, 0:CONV_HALO, :] = zeros[0:CONV_HALO]
        pad_ref[res, s_len + 8:s_len + 2 * CONV_HALO, :] = zeros
    for res in range(8):
        pad_ref[res, CONV_HALO - res:CONV_HALO - res + s_len, :] = z
    bias = b_ref[...]
    gam = g_ref[...]
    beta = beta_ref[...]
    off0 = CONV_HALO - CONV_KERNEL // 2

    def chunk(c, carry):
        r0 = pl.multiple_of(c * CONV_CHUNK, CONV_CHUNK)
        slabs = []
        for s0 in range(0, CONV_WIDTH, LANES):
            sl = slice(s0, s0 + LANES)
            part = jnp.broadcast_to(bias[:, sl], (CONV_CHUNK, LANES))
            for j in range(CONV_KERNEL):
                res = (off0 + j) % 8
                a8 = off0 + j - res
                part = part + pad_ref[res, pl.ds(r0 + a8, CONV_CHUNK), sl] * w_ref[j:j + 1, sl]
            slabs.append(part)
        acc = jnp.concatenate(slabs, axis=-1)
        mu = jnp.mean(acc, axis=-1, keepdims=True)
        d = acc - mu
        var = jnp.mean(d * d, axis=-1, keepdims=True)
        y = d * lax.rsqrt(var + EPS) * gam + beta
        o_ref[pl.ds(r0, CONV_CHUNK), :] = (y * jax.nn.sigmoid(y)).astype(BF16)
        return carry

    lax.fori_loop(0, s_len // CONV_CHUNK, chunk, 0)


def _conv_module(z, conv_w_l, conv_b_l, g_l, beta_l, n_batch, s_len, row_blk0):
    const2 = lambda b: (0, 0)
    return pl.pallas_call(
        functools.partial(_conv_kernel, s_len=s_len),
        grid=(n_batch,),
        in_specs=[
            pl.BlockSpec((s_len, CONV_WIDTH), lambda b: (row_blk0 + b, 0)),
            pl.BlockSpec((CONV_KERNEL, CONV_WIDTH), const2),
            pl.BlockSpec((1, CONV_WIDTH), const2),
            pl.BlockSpec((1, CONV_WIDTH), const2),
            pl.BlockSpec((1, CONV_WIDTH), const2),
        ],
        out_specs=pl.BlockSpec((s_len, CONV_WIDTH), lambda b: (b, 0)),
        out_shape=jax.ShapeDtypeStruct((n_batch * s_len, CONV_WIDTH), BF16),
        scratch_shapes=[pltpu.VMEM((8, s_len + 2 * CONV_HALO, CONV_WIDTH), F32)],
        compiler_params=_cparams(1),
        name="conv",
    )(z, conv_w_l, conv_b_l, g_l, beta_l)


def _route_rows(logits, seen, tri):
    rows = logits.shape[0]
    lane = lax.broadcasted_iota(jnp.int32, (rows, LANES), 1)
    lane_f = lane.astype(F32)
    work = logits
    sel_masks, top_vals, top_idx = [], [], []
    for _ in range(TOP_K):
        mx = jnp.max(work, axis=-1, keepdims=True)
        idx = jnp.min(jnp.where(work == mx, lane_f, float(LANES)), axis=-1, keepdims=True)
        sel = lane_f == idx
        sel_masks.append(sel)
        top_vals.append(mx)
        top_idx.append(idx)
        work = jnp.where(sel, NEG_BIG * 2.0, work)
    exps = [jnp.exp(v - top_vals[0]) for v in top_vals]
    denom = exps[0] + exps[1] + exps[2] + exps[3]
    onehot = jnp.zeros((rows, LANES), F32)
    for sel in sel_masks:
        onehot = onehot + jnp.where(sel, 1.0, 0.0)
    base = jnp.dot(tri, onehot.astype(BF16), preferred_element_type=F32) + seen
    te = jnp.zeros((rows, LANES), F32)
    gw = jnp.zeros((rows, LANES), F32)
    rk = jnp.zeros((rows, LANES), F32)
    for kk in range(TOP_K):
        r = jnp.sum(jnp.where(sel_masks[kk], base, 0.0), axis=-1, keepdims=True)
        col = lane == kk
        te = jnp.where(col, top_idx[kk], te)
        gw = jnp.where(col, exps[kk] / denom, gw)
        rk = jnp.where(col, r, rk)
    return te.astype(jnp.int32), gw, rk.astype(jnp.int32), jnp.sum(onehot, axis=0, keepdims=True)


def _outproj_kernel(x_ref, op_ref, os_ref, zp_ref, zs_ref, gate_ref, mod_ref, wa_ref, wc_ref, wo_ref,
                    gffn_ref, wrh_ref, wrl_ref, br_ref, tri_ref,
                    x1_ref, h2_ref, te_ref, gw_ref, rank_ref, cnt_ref, carry_ref, *, n_ctx_tiles):
    i = pl.program_id(0)

    @pl.when(i == 0)
    def _():
        carry_ref[...] = jnp.zeros_like(carry_ref)

    m = mod_ref[0]
    is_ctx = i < n_ctx_tiles
    seen = carry_ref[...]
    for g0 in range(0, TMO, TM):
        rs = slice(g0, g0 + TM)
        o = jnp.where(is_ctx, op_ref[rs, :], os_ref[rs, :])
        zc = jnp.where(is_ctx, zp_ref[rs, :], zs_ref[rs, :])
        attn_d = jnp.dot(o, wa_ref[...], preferred_element_type=F32)
        conv_d = jnp.dot(zc, wc_ref[...], preferred_element_type=F32)
        y = (gate_ref[rs, 0:D_MODEL].astype(F32) * attn_d
             + gate_ref[rs, D_MODEL:2 * D_MODEL].astype(F32) * conv_d)
        mix = jnp.dot(y.astype(BF16), wo_ref[...], preferred_element_type=F32)
        x1 = x_ref[rs, :] + m[2:3] * mix
        x1_ref[rs, :] = x1
        h2 = _rms_mod(x1, gffn_ref[...], m[3:4], m[4:5])
        h2_ref[rs, :] = h2
        h_hi = h2.astype(BF16)
        h_lo = (h2 - h_hi.astype(F32)).astype(BF16)
        logits = (jnp.dot(h_hi, wrh_ref[...], preferred_element_type=F32)
                  + (jnp.dot(h_lo, wrh_ref[...], preferred_element_type=F32)
                     + jnp.dot(h_hi, wrl_ref[...], preferred_element_type=F32))) + br_ref[...]
        te, gw, rk, cnt = _route_rows(logits, seen, tri_ref[...])
        te_ref[rs, :] = te
        gw_ref[rs, :] = gw
        rank_ref[rs, :] = rk
        seen = seen + cnt
    carry_ref[...] = seen
    cnt_ref[...] = jnp.broadcast_to(seen, cnt_ref.shape).astype(jnp.int32)


def _outproj(x, o_p, o_s, zc_p, zc_s, gates, mod_l, wa_bf, wc_bf, wo_bf, l, gffn, wr_hi, wr_lo, br_pad,
             tri, s_lat):
    n_tok = x.shape[0]
    n_ctx_tiles = o_p.shape[0] // TMO
    n_lat_tiles = o_s.shape[0] // TMO
    row = lambda i: (i, 0)
    ctxrow = lambda i: (jnp.minimum(i, n_ctx_tiles - 1), 0)
    latrow = lambda i: (jnp.clip(i - n_ctx_tiles, 0, n_lat_tiles - 1), 0)
    const2 = lambda i: (0, 0)
    lay3 = lambda i: (l, 0, 0)
    modrow = lambda i: (_mod_row_of_tile(i, n_ctx_tiles, s_lat // TMO), 0, 0)
    return pl.pallas_call(
        functools.partial(_outproj_kernel, n_ctx_tiles=n_ctx_tiles),
        grid=(n_tok // TMO,),
        in_specs=[
            pl.BlockSpec((TMO, D_MODEL), row),
            pl.BlockSpec((TMO, ATTN_WIDTH), ctxrow),
            pl.BlockSpec((TMO, ATTN_WIDTH), latrow),
            pl.BlockSpec((TMO, CONV_WIDTH), ctxrow),
            pl.BlockSpec((TMO, CONV_WIDTH), latrow),
            pl.BlockSpec((TMO, 2 * D_MODEL), row),
            pl.BlockSpec((1, 6, D_MODEL), modrow),
            pl.BlockSpec((None, ATTN_WIDTH, D_MODEL), lay3),
            pl.BlockSpec((None, CONV_WIDTH, D_MODEL), lay3),
            pl.BlockSpec((None, D_MODEL, D_MODEL), lay3),
            pl.BlockSpec((1, D_MODEL), const2),
            pl.BlockSpec((D_MODEL, LANES), const2),
            pl.BlockSpec((D_MODEL, LANES), const2),
            pl.BlockSpec((1, LANES), const2),
            pl.BlockSpec((TM, TM), const2),
        ],
        out_specs=[
            pl.BlockSpec((TMO, D_MODEL), row),
            pl.BlockSpec((TMO, D_MODEL), row),
            pl.BlockSpec((TMO, LANES), row),
            pl.BlockSpec((TMO, LANES), row),
            pl.BlockSpec((TMO, LANES), row),
            pl.BlockSpec((8, LANES), const2),
        ],
        out_shape=[
            jax.ShapeDtypeStruct((n_tok, D_MODEL), F32),
            jax.ShapeDtypeStruct((n_tok, D_MODEL), F32),
            jax.ShapeDtypeStruct((n_tok, LANES), jnp.int32),
            jax.ShapeDtypeStruct((n_tok, LANES), F32),
            jax.ShapeDtypeStruct((n_tok, LANES), jnp.int32),
            jax.ShapeDtypeStruct((8, LANES), jnp.int32),
        ],
        scratch_shapes=[pltpu.VMEM((1, LANES), F32)],
        compiler_params=_cparams(1),
        name="outproj_router",
    )(x, o_p, o_s, zc_p, zc_s, gates, mod_l, wa_bf, wc_bf, wo_bf, gffn, wr_hi, wr_lo, br_pad, tri)


def _invert_kernel(dest_ref, init_ref, inv_ref, sem):
    i = pl.program_id(0)

    @pl.when(i == 0)
    def _():
        cp = pltpu.make_async_copy(init_ref, inv_ref, sem)
        cp.start()
        cp.wait()

    base = i * INV_CHUNK

    def body(a, carry):
        inv_ref[dest_ref[a]] = base + a
        return carry

    lax.fori_loop(0, INV_CHUNK, body, 0, unroll=16)


def _invert(dest_flat, inv_init):
    return pl.pallas_call(
        _invert_kernel,
        grid=(dest_flat.shape[0] // INV_CHUNK,),
        in_specs=[
            pl.BlockSpec((INV_CHUNK,), lambda i: (i,), memory_space=pltpu.SMEM),
            pl.BlockSpec(memory_space=pl.ANY),
        ],
        out_specs=pl.BlockSpec(memory_space=pltpu.SMEM),
        out_shape=jax.ShapeDtypeStruct(inv_init.shape, jnp.int32),
        scratch_shapes=[pltpu.SemaphoreType.DMA(())],
        compiler_params=_cparams(1),
        name="moe_invert",
    )(dest_flat, inv_init)


def _expert_kernel(be_ref, nused_ref, par_ref, nxt_ref, inv_ref, h_ref, wgu_hbm, bgu_ref, wd_hbm, bd_ref,
                   y_ref, wgu_f32, wd_f32, wgu_bf, wd_bf, xb0, xb1, xb2, gsem, wsem, *, n_blk, layer):
    j = pl.program_id(0)
    n_used = nused_ref[0]
    xbuf = (xb0, xb1, xb2)

    def weight_copies(e, p):
        return (pltpu.make_async_copy(wgu_hbm.at[layer, e], wgu_f32.at[p], wsem.at[p]),
                pltpu.make_async_copy(wd_hbm.at[layer, e], wd_f32.at[p], wsem.at[p]))

    def gather_copy(r, slot):
        tok = inv_ref[r] >> 2
        return pltpu.make_async_copy(h_ref.at[pl.ds(tok, 1), :], xbuf[slot].at[pl.ds(r, 1), :],
                                     gsem.at[slot])

    def wait_gather(slot):
        pltpu.make_async_copy(h_ref.at[pl.ds(0, EBLK), :], xbuf[slot], gsem.at[slot]).wait()

    def rolled_gather(slot):
        def body(r, carry):
            gather_copy(r, slot).start()
            return carry
        lax.fori_loop(0, EBLK, body, 0)

    @pl.when(j == 0)
    def _():
        for cp in weight_copies(be_ref[0], 0):
            cp.start()
        rolled_gather(0)

    @pl.when(j == 1)
    def _():
        rolled_gather(1)

    blk = j - 2
    is_compute = jnp.logical_and(j >= 2, j <= n_used + 1)
    bclip = jnp.clip(blk, 0, n_blk - 1)
    e_cur = be_ref[bclip]
    e_prev = be_ref[jnp.clip(blk - 1, 0, n_blk - 1)]

    @pl.when(jnp.logical_and(is_compute, jnp.logical_or(blk == 0, e_cur != e_prev)))
    def _():
        p = par_ref[bclip]
        e_next = nxt_ref[bclip]
        for cp in weight_copies(e_cur, p):
            cp.wait()

        @pl.when(e_next >= 0)
        def _():
            for cp in weight_copies(e_next, 1 - p):
                cp.start()

        wgu_bf[...] = wgu_f32[p].astype(BF16)
        wd_bf[...] = wd_f32[p].astype(BF16)

    for slot in range(NBUF):
        nxt = (slot + 1) % NBUF
        prv = (slot + 2) % NBUF

        @pl.when(jnp.logical_and(is_compute, blk % NBUF == slot))
        def _():
            wait_gather(slot)
            for r in range(EBLK):
                gather_copy(r, prv).start()
            x = xbuf[slot][...].astype(BF16)
            gu = jnp.dot(x, wgu_bf[...], preferred_element_type=F32) + bgu_ref[...]
            g = jnp.minimum(gu[:, 0:D_EXPERT], SWIGLU_LIMIT)
            u = jnp.clip(gu[:, D_EXPERT:2 * D_EXPERT], -SWIGLU_LIMIT, SWIGLU_LIMIT)
            act = (u + 1.0) * (g * jax.nn.sigmoid(SWIGLU_ALPHA * g))
            y_ref[...] = jnp.dot(act.astype(BF16), wd_bf[...], preferred_element_type=F32) + bd_ref[...]

        @pl.when(jnp.logical_and(j == n_used + 2, (n_used - 1) % NBUF == slot))
        def _():
            wait_gather(nxt)
            wait_gather(prv)


def _experts(h2, inv, block_e, n_used, run_par, run_next, w_gate_up, b_gate_up, w_down, b_down, l):
    n_rows = inv.shape[0]
    n_blk = n_rows // EBLK
    cur = lambda j, nu: jnp.clip(j - 2, 0, nu[0] - 1)
    bblk = lambda j, be, nu, rp, rn: (l, be[cur(j, nu)], 0, 0)
    tile = pltpu.VMEM((EBLK, D_MODEL), F32)
    grid_spec = pltpu.PrefetchScalarGridSpec(
        num_scalar_prefetch=4,
        grid=(n_blk + 3,),
        in_specs=[
            pl.BlockSpec((EBLK,), lambda j, be, nu, rp, rn: (jnp.minimum(j, n_blk - 1),),
                         memory_space=pltpu.SMEM),
            pl.BlockSpec(memory_space=pl.ANY),
            pl.BlockSpec(memory_space=pl.ANY),
            pl.BlockSpec((None, None, 1, 2 * D_EXPERT), bblk),
            pl.BlockSpec(memory_space=pl.ANY),
            pl.BlockSpec((None, None, 1, D_MODEL), bblk),
        ],
        out_specs=pl.BlockSpec((EBLK, D_MODEL), lambda j, be, nu, rp, rn: (cur(j, nu), 0)),
        scratch_shapes=[
            pltpu.VMEM((2, D_MODEL, 2 * D_EXPERT), F32),
            pltpu.VMEM((2, D_EXPERT, D_MODEL), F32),
            pltpu.VMEM((D_MODEL, 2 * D_EXPERT), BF16),
            pltpu.VMEM((D_EXPERT, D_MODEL), BF16),
            tile, tile, tile,
            pltpu.SemaphoreType.DMA((NBUF,)),
            pltpu.SemaphoreType.DMA((2,)),
        ],
    )
    return pl.pallas_call(
        functools.partial(_expert_kernel, n_blk=n_blk, layer=l),
        grid_spec=grid_spec,
        out_shape=jax.ShapeDtypeStruct((n_rows, D_MODEL), F32),
        compiler_params=_cparams(1),
        name="moe_experts",
    )(block_e, n_used, run_par, run_next, inv, h2, w_gate_up,
      b_gate_up.reshape(DEPTH, N_EXPERTS, 1, 2 * D_EXPERT), w_down,
      b_down.reshape(DEPTH, N_EXPERTS, 1, D_MODEL))


def _combine_kernel(dcur_ref, dnxt_ref, x1_ref, mod_ref, gw_ref, yb_ref, o_ref, buf0, buf1, sem):
    i = pl.program_id(0)
    bufs = (buf0, buf1)

    def row_copy(dest_ref, r, kk, slot):
        return pltpu.make_async_copy(yb_ref.at[pl.ds(dest_ref[r * TOP_K + kk], 1), :],
                                     bufs[slot].at[kk, pl.ds(r, 1), :], sem.at[slot])

    def wait_tile(slot):
        for kk in range(TOP_K):
            pltpu.make_async_copy(yb_ref.at[pl.ds(0, TM), :], bufs[slot].at[kk], sem.at[slot]).wait()

    @pl.when(i == 0)
    def _():
        def body(r, carry):
            for kk in range(TOP_K):
                row_copy(dcur_ref, r, kk, 0).start()
            return carry
        lax.fori_loop(0, TM, body, 0)

    for slot in range(2):
        @pl.when(i % 2 == slot)
        def _():
            @pl.when(i + 1 < pl.num_programs(0))
            def _():
                for r in range(TM):
                    for kk in range(TOP_K):
                        row_copy(dnxt_ref, r, kk, 1 - slot).start()

            wait_tile(slot)
            gw = gw_ref[...]
            y = gw[:, 0:1] * bufs[slot][0]
            for kk in range(1, TOP_K):
                y = y + gw[:, kk:kk + 1] * bufs[slot][kk]
            o_ref[...] = x1_ref[...] + mod_ref[0][5:6] * y


def _combine(x1, mod_l, gatew, dest_flat, yb, n_ctx_tiles, tiles_per_seq):
    n_tok = x1.shape[0]
    n_tiles = n_tok // TM
    row = lambda i: (i, 0)
    modrow = lambda i: (_mod_row_of_tile(i, n_ctx_tiles, tiles_per_seq), 0, 0)
    gbuf = pltpu.VMEM((TOP_K, TM, D_MODEL), F32)
    return pl.pallas_call(
        _combine_kernel,
        grid=(n_tiles,),
        in_specs=[
            pl.BlockSpec((TM * TOP_K,), lambda i: (i,), memory_space=pltpu.SMEM),
            pl.BlockSpec((TM * TOP_K,), lambda i: (jnp.minimum(i + 1, n_tiles - 1),),
                         memory_space=pltpu.SMEM),
            pl.BlockSpec((TM, D_MODEL), row),
            pl.BlockSpec((1, 6, D_MODEL), modrow),
            pl.BlockSpec((TM, LANES), row),
            pl.BlockSpec(memory_space=pl.ANY),
        ],
        out_specs=pl.BlockSpec((TM, D_MODEL), row),
        out_shape=jax.ShapeDtypeStruct((n_tok, D_MODEL), F32),
        scratch_shapes=[gbuf, gbuf, pltpu.SemaphoreType.DMA((2,))],
        compiler_params=_cparams(1),
        name="moe_combine",
    )(dest_flat, dest_flat, x1, mod_l, gatew, yb)


def _rope_tables(s_len):
    rows = s_len // GRID_W
    row = jnp.repeat(jnp.arange(rows, dtype=F32), GRID_W)
    col = jnp.tile(jnp.arange(GRID_W, dtype=F32), rows)
    n_freq = HEAD_DIM // 4
    inv = ROPE_BASE ** (-jnp.arange(n_freq, dtype=F32) / n_freq)
    ang_r = row[:, None] * inv
    ang_c = col[:, None] * inv
    ang = jnp.concatenate([ang_r, ang_r, ang_c, ang_c], axis=1)
    sign = jnp.tile(jnp.concatenate([-jnp.ones((n_freq,), F32), jnp.ones((n_freq,), F32)]), 2)
    cos = jnp.tile(jnp.cos(ang), (1, 2))
    sin = jnp.tile(jnp.sin(ang) * sign, (1, 2))
    cos = jnp.concatenate([jnp.ones((TM, LANES), F32), cos], axis=0)
    sin = jnp.concatenate([jnp.zeros((TM, LANES), F32), sin], axis=0)
    return cos, sin


def kernel(x_prompt, x_sample, c, cache_k, cache_v, c_ctx, w_ada, b_ada, norm_mix, norm_ffn, w_in, q_norm, k_norm, lam_qk, sub_norm, w_attn_out, conv_w, conv_b, conv_norm_g, conv_norm_b, w_conv_out, w_out, w_router, b_router, w_gate_up, b_gate_up, w_down, b_down):
    b_p, s_p, _ = x_prompt.shape
    b_s, s_s, _ = x_sample.shape
    past = cache_k.shape[2]
    n_ctx = b_p * s_p
    n_tok = n_ctx + b_s * s_s
    n_ctx_tiles = n_ctx // TM
    tiles_per_seq = s_s // TM
    assert s_p == TM and s_s % TMO == 0 and n_ctx % s_s == 0 and s_s % GRID_W == 0
    assert (n_tok * TOP_K) % INV_CHUNK == 0 and (n_tok * TOP_K) % EBLK == 0

    x = jnp.concatenate([x_prompt.reshape(n_ctx, D_MODEL), x_sample.reshape(b_s * s_s, D_MODEL)], axis=0)
    n_cond = 8 * pl.cdiv(1 + b_s, 8)
    cond = jnp.zeros((n_cond, D_MODEL), F32).at[0].set(c_ctx).at[1:1 + b_s].set(c)
    mod = _ada_all_layers(cond, w_ada.astype(BF16), b_ada).reshape(DEPTH, n_cond, 6, D_MODEL)

    w_in_bf = w_in.astype(BF16)
    wa_bf = w_attn_out.astype(BF16)
    wc_bf = w_conv_out.astype(BF16)
    wo_bf = w_out.astype(BF16)
    ck = cache_k.reshape(b_s, DEPTH, past, ATTN_WIDTH)
    cv = cache_v.reshape(b_s, DEPTH, past, ATTN_WIDTH)
    cos_t, sin_t = _rope_tables(s_s)
    grp = jnp.arange(ATTN_WIDTH, dtype=jnp.int32) // HEAD_DIM
    gsum = (grp[:, None] == grp[None, :]).astype(BF16)
    tri = (jnp.arange(TM)[:, None] > jnp.arange(TM)[None, :]).astype(BF16)
    n_blk = (n_tok * TOP_K) // EBLK + N_EXPERTS
    n_rows = n_blk * EBLK
    blk_row0 = jnp.arange(n_blk, dtype=jnp.int32) * EBLK
    inv_init = jnp.zeros((n_rows,), jnp.int32)

    new_k, new_v = [], []
    for l in range(DEPTH):
        lam_init = 0.8 - 0.6 * math.exp(-0.3 * l)
        q, k, v, zpre, gates = _inproj(
            x, mod[l], norm_mix[l].reshape(1, D_MODEL), w_in_bf, l,
            jnp.tile(q_norm[l], ATTN_WIDTH // HEAD_DIM).reshape(1, ATTN_WIDTH),
            jnp.tile(k_norm[l], ATTN_WIDTH // HEAD_DIM).reshape(1, ATTN_WIDTH),
            gsum, cos_t, sin_t, n_ctx_tiles, tiles_per_seq)
        new_k.append(k[:n_ctx].reshape(b_p, s_p, N_HEADS, 2 * HEAD_DIM))
        new_v.append(v[:n_ctx].reshape(b_p, s_p, N_HEADS, 2 * HEAD_DIM))
        sub_l = sub_norm[l].reshape(1, LANES)
        o_p = _attention(q, k, v, lam_qk[l], sub_l, lam_init, b_p, s_p, 0)
        o_s = _attention(q, k, v, lam_qk[l], sub_l, lam_init, b_s, s_s, n_ctx // s_s, ctx=(ck, cv, l))
        conv_args = (conv_w[l], conv_b[l].reshape(1, CONV_WIDTH), conv_norm_g[l].reshape(1, CONV_WIDTH),
                     conv_norm_b[l].reshape(1, CONV_WIDTH))
        zc_p = _conv_module(zpre, *conv_args, b_p, s_p, 0)
        zc_s = _conv_module(zpre, *conv_args, b_s, s_s, n_ctx // s_s)
        wr_pad = jnp.zeros((D_MODEL, LANES), F32).at[:, :N_EXPERTS].set(w_router[l])
        wr_hi = wr_pad.astype(BF16)
        wr_lo = (wr_pad - wr_hi.astype(F32)).astype(BF16)
        br_pad = jnp.full((1, LANES), NEG_BIG, F32).at[0, :N_EXPERTS].set(b_router[l])
        x1, h2, top_e, gatew, rank, counts = _outproj(
            x, o_p, o_s, zc_p, zc_s, gates, mod[l], wa_bf, wc_bf, wo_bf, l,
            norm_ffn[l].reshape(1, D_MODEL), wr_hi, wr_lo, br_pad, tri, s_s)

        cnt = counts[0, :N_EXPERTS]
        padded = (cnt + EBLK - 1) // EBLK * EBLK
        pend = jnp.cumsum(padded)
        pstart = pend - padded
        block_e = jnp.minimum(jnp.sum((pend[None, :] <= blk_row0[:, None]).astype(jnp.int32), axis=1),
                              N_EXPERTS - 1)
        n_used = (pend[-1] // EBLK).astype(jnp.int32).reshape(1)
        has = (cnt > 0).astype(jnp.int32)
        run_idx = jnp.cumsum(has) - has
        eid = jnp.arange(N_EXPERTS, dtype=jnp.int32)
        later = jnp.where((eid[None, :] > eid[:, None]) & (has[None, :] > 0), eid[None, :], N_EXPERTS)
        next_e = jnp.min(later, axis=1)
        next_e = jnp.where(next_e < N_EXPERTS, next_e, -1).astype(jnp.int32)
        run_par = (run_idx % 2).astype(jnp.int32)[block_e]
        run_next = next_e[block_e]

        dest = (pstart[top_e[:, :TOP_K]] + rank[:, :TOP_K]).reshape(n_tok * TOP_K)
        inv = _invert(dest, inv_init)
        yb = _experts(h2, inv, block_e, n_used, run_par, run_next, w_gate_up, b_gate_up, w_down, b_down, l)
        x = _combine(x1, mod[l], gatew, dest, yb, n_ctx_tiles, tiles_per_seq)

    y_prompt = x[:n_ctx].reshape(b_p, s_p, D_MODEL)
    y_sample = x[n_ctx:].reshape(b_s, s_s, D_MODEL)
    return (y_prompt, y_sample, jnp.stack(new_k, axis=1), jnp.stack(new_v, axis=1))
```

```python
import functools
import math

import jax
import jax.numpy as jnp
from jax import lax
from jax.experimental import pallas as pl
from jax.experimental.pallas import tpu as pltpu

F32 = jnp.float32
BF16 = jnp.bfloat16

D_MODEL = 1024
DEPTH = 4
GRID_W = 64
ATTN_WIDTH = 512
N_HEADS = 4
HEAD_DIM = 64
CONV_WIDTH = 512
CONV_KERNEL = 31
N_EXPERTS = 32
TOP_K = 4
D_EXPERT = 1024
SWIGLU_LIMIT = 7.0
SWIGLU_ALPHA = 1.702
ROPE_BASE = 10000.0
EPS = 1e-6

LANES = 128
TM = 256
TMO = 512
EBLK = 256
INV_CHUNK = 4096
NBUF = 3
ADA_TN = 1024
CONV_CHUNK = 64
CONV_HALO = 16
NEG_BIG = -1e30
VMEM_LIMIT = 56 * 1024 * 1024


def _cparams(n_axes, vmem=None):
    return pltpu.CompilerParams(
        dimension_semantics=("arbitrary",) * n_axes,
        vmem_limit_bytes=VMEM_LIMIT if vmem is None else vmem)


def _token_stream_specs(x, tile, n_ctx_tiles):
    if isinstance(x, tuple):
        xs = x
        n_tok = x[0].shape[0] + x[1].shape[0]
        lat0 = n_ctx_tiles
    else:
        xs = (x, x)
        n_tok = x.shape[0]
        lat0 = 0
    n_tiles = n_tok // tile
    specs = [pl.BlockSpec((tile, D_MODEL), lambda i: (jnp.minimum(i, n_ctx_tiles - 1), 0)),
             pl.BlockSpec((tile, D_MODEL), lambda i: (jnp.clip(i, n_ctx_tiles, n_tiles - 1) - lat0, 0))]
    return xs, n_tok, specs


def _mod_row_of_tile(i, n_ctx_tiles, tiles_per_seq):
    return jnp.where(i < n_ctx_tiles, 0, 1 + (i - n_ctx_tiles) // tiles_per_seq)


def _ada_kernel(cond_ref, w_ref, b_ref, o_ref):
    cnd = cond_ref[...]
    act = cnd * jax.nn.sigmoid(cnd)
    o_ref[0] = jnp.dot(act.astype(BF16), w_ref[0].astype(BF16), preferred_element_type=F32) + b_ref[0]


def _ada_all_layers(cond, w_ada, b_ada):
    n_rows = cond.shape[0]
    n_col = w_ada.shape[2]
    return pl.pallas_call(
        _ada_kernel,
        grid=(DEPTH, n_col // ADA_TN),
        in_specs=[
            pl.BlockSpec((n_rows, D_MODEL), lambda l, j: (0, 0)),
            pl.BlockSpec((1, D_MODEL, ADA_TN), lambda l, j: (l, 0, j)),
            pl.BlockSpec((1, 1, ADA_TN), lambda l, j: (l, 0, j)),
        ],
        out_specs=pl.BlockSpec((1, n_rows, ADA_TN), lambda l, j: (l, 0, j)),
        out_shape=jax.ShapeDtypeStruct((DEPTH, n_rows, n_col), F32),
        compiler_params=_cparams(2),
        name="ada",
    )(cond, w_ada, b_ada.reshape(DEPTH, 1, n_col))


def _rms_mod(x, g, shift, scale):
    y = x * lax.rsqrt(jnp.mean(x * x, axis=-1, keepdims=True) + EPS) * g
    return y * (1.0 + scale) + shift


def _group_mean_sq(t, gsum):
    s = t * t
    hi = s.astype(BF16)
    lo = (s - hi.astype(F32)).astype(BF16)
    tot = jnp.dot(hi, gsum, preferred_element_type=F32) + jnp.dot(lo, gsum, preferred_element_type=F32)
    return tot * (1.0 / HEAD_DIM)


def _rope_slab(x, cos, sgn_sin, first_half):
    partner = jnp.where(first_half, pltpu.roll(x, LANES - 16, 1), pltpu.roll(x, 16, 1))
    return x * cos + partner * sgn_sin


def _inproj_kernel(xc_ref, xl_ref, mod_ref, gmix_ref, w_ref, qn_ref, kn_ref, gsum_ref, cos_ref, sin_ref,
                   q_ref, k_ref, v_ref, z_ref, gate_ref, *, n_ctx_tiles):
    x = jnp.where(pl.program_id(0) < n_ctx_tiles, xc_ref[...], xl_ref[...])
    m = mod_ref[0]
    h = _rms_mod(x, gmix_ref[...], m[0:1], m[1:2]).astype(BF16)
    aw = ATTN_WIDTH
    q = jnp.dot(h, w_ref[:, 0:aw], preferred_element_type=F32)
    k = jnp.dot(h, w_ref[:, aw:2 * aw], preferred_element_type=F32)
    gsum = gsum_ref[...]
    q = q * lax.rsqrt(_group_mean_sq(q, gsum) + EPS) * qn_ref[...]
    k = k * lax.rsqrt(_group_mean_sq(k, gsum) + EPS) * kn_ref[...]
    cos = cos_ref[...]
    sin = sin_ref[...]
    lane = lax.broadcasted_iota(jnp.int32, (TM, LANES), 1)
    first_half = (lane & 16) == 0
    for hd in range(N_HEADS):
        sl = slice(hd * LANES, (hd + 1) * LANES)
        q_ref[:, sl] = (_rope_slab(q[:, sl], cos, sin, first_half) * (HEAD_DIM ** -0.5)).astype(BF16)
        k_ref[:, sl] = _rope_slab(k[:, sl], cos, sin, first_half)
    v_ref[...] = jnp.dot(h, w_ref[:, 2 * aw:3 * aw], preferred_element_type=F32)
    c0 = 3 * aw
    a = jnp.dot(h, w_ref[:, c0:c0 + CONV_WIDTH], preferred_element_type=F32)
    g = jnp.dot(h, w_ref[:, c0 + CONV_WIDTH:c0 + 2 * CONV_WIDTH], preferred_element_type=F32)
    z_ref[...] = a * jax.nn.sigmoid(g)
    c1 = c0 + 2 * CONV_WIDTH
    gl = jnp.dot(h, w_ref[:, c1:c1 + 2 * D_MODEL], preferred_element_type=F32)
    gate_ref[...] = jax.nn.sigmoid(gl).astype(BF16)


def _inproj(x, mod_l, gmix, w_in_bf, l, qn, kn, gsum, cos_t, sin_t, n_ctx_tiles, tiles_per_seq):
    xs, n_tok, x_specs = _token_stream_specs(x, TM, n_ctx_tiles)
    in_w = w_in_bf.shape[2]
    row = lambda i: (i, 0)
    modrow = lambda i: (_mod_row_of_tile(i, n_ctx_tiles, tiles_per_seq), 0, 0)
    ropeblk = lambda i: (jnp.where(i < n_ctx_tiles, 0, 1 + (i - n_ctx_tiles) % tiles_per_seq), 0)
    const2 = lambda i: (0, 0)
    return pl.pallas_call(
        functools.partial(_inproj_kernel, n_ctx_tiles=n_ctx_tiles),
        grid=(n_tok // TM,),
        in_specs=x_specs + [
            pl.BlockSpec((1, 6, D_MODEL), modrow),
            pl.BlockSpec((1, D_MODEL), const2),
            pl.BlockSpec((None, D_MODEL, in_w), lambda i: (l, 0, 0)),
            pl.BlockSpec((1, ATTN_WIDTH), const2),
            pl.BlockSpec((1, ATTN_WIDTH), const2),
            pl.BlockSpec((ATTN_WIDTH, ATTN_WIDTH), const2),
            pl.BlockSpec((TM, LANES), ropeblk),
            pl.BlockSpec((TM, LANES), ropeblk),
        ],
        out_specs=[
            pl.BlockSpec((TM, ATTN_WIDTH), row),
            pl.BlockSpec((TM, ATTN_WIDTH), row),
            pl.BlockSpec((TM, ATTN_WIDTH), row),
            pl.BlockSpec((TM, CONV_WIDTH), row),
            pl.BlockSpec((TM, 2 * D_MODEL), row),
        ],
        out_shape=[
            jax.ShapeDtypeStruct((n_tok, ATTN_WIDTH), BF16),
            jax.ShapeDtypeStruct((n_tok, ATTN_WIDTH), F32),
            jax.ShapeDtypeStruct((n_tok, ATTN_WIDTH), F32),
            jax.ShapeDtypeStruct((n_tok, CONV_WIDTH), F32),
            jax.ShapeDtypeStruct((n_tok, 2 * D_MODEL), BF16),
        ],
        compiler_params=_cparams(1),
        name="inproj",
    )(*xs, mod_l, gmix, w_in_bf, qn, kn, gsum, cos_t, sin_t)


def _attn_kernel(*refs, lam_init, s_q, s_ctx, q_chunk):
    if s_ctx:
        q_ref, k_ref, v_ref, ck_ref, cv_ref, lam_ref, sub_ref, o_ref, kc_ref, vc_ref = refs
        kc_ref[0:s_ctx, :] = ck_ref[...].astype(BF16)
        vc_ref[0:s_ctx, :] = cv_ref[...].astype(BF16)
    else:
        q_ref, k_ref, v_ref, lam_ref, sub_ref, o_ref, kc_ref, vc_ref = refs
    kc_ref[s_ctx:s_ctx + s_q, :] = k_ref[...].astype(BF16)
    vc_ref[s_ctx:s_ctx + s_q, :] = v_ref[...].astype(BF16)
    lq = lam_ref[...]
    lam = (jnp.exp(jnp.sum(lq[0:1] * lq[1:2], axis=-1, keepdims=True))
           - jnp.exp(jnp.sum(lq[2:3] * lq[3:4], axis=-1, keepdims=True)) + lam_init)
    lane = lax.broadcasted_iota(jnp.int32, (q_chunk, LANES), 1)
    map0 = lane < HEAD_DIM
    sub = sub_ref[...]
    nt = (((1,), (1,)), ((), ()))

    def chunk(c, carry):
        r0 = pl.multiple_of(c * q_chunk, q_chunk)
        q = q_ref[pl.ds(r0, q_chunk), :]
        kc = kc_ref[...]
        zero = jnp.zeros_like(q)
        exps, inv_sums = [], []
        for mp in range(2):
            qm = jnp.where(map0, q, zero) if mp == 0 else jnp.where(map0, zero, q)
            s = lax.dot_general(qm, kc, nt, preferred_element_type=F32)
            e = jnp.exp(s - jnp.max(s, axis=-1, keepdims=True))
            inv_sums.append(1.0 / jnp.sum(e, axis=-1, keepdims=True))
            exps.append(e.astype(BF16))
        pv = jnp.dot(jnp.concatenate(exps, axis=0), vc_ref[...], preferred_element_type=F32)
        o = pv[0:q_chunk] * inv_sums[0] - lam * (pv[q_chunk:2 * q_chunk] * inv_sums[1])
        o = o * lax.rsqrt(jnp.mean(o * o, axis=-1, keepdims=True) + EPS) * sub * (1.0 - lam_init)
        o_ref[pl.ds(r0, q_chunk), :] = o.astype(BF16)
        return carry

    lax.fori_loop(0, s_q // q_chunk, chunk, 0, unroll=min(2, s_q // q_chunk))


def _attention(q, k, v, lam_qk_l, sub_l, lam_init, n_batch, s_q, row_blk0, ctx=None):
    s_ctx = 0 if ctx is None else ctx[0].shape[2]
    own = pl.BlockSpec((s_q, LANES), lambda b, h: (row_blk0 + b, h))
    in_specs = [own, own, own]
    args = [q, k, v]
    if ctx is not None:
        ck, cv, l = ctx
        cspec = pl.BlockSpec((None, None, s_ctx, LANES), lambda b, h: (b, l, 0, h))
        in_specs += [cspec, cspec]
        args += [ck, cv]
    in_specs += [pl.BlockSpec((4, HEAD_DIM), lambda b, h: (0, 0)),
                 pl.BlockSpec((1, LANES), lambda b, h: (0, 0))]
    args += [lam_qk_l, sub_l]
    return pl.pallas_call(
        functools.partial(_attn_kernel, lam_init=lam_init, s_q=s_q, s_ctx=s_ctx, q_chunk=min(s_q, 256)),
        grid=(n_batch, N_HEADS),
        in_specs=in_specs,
        out_specs=pl.BlockSpec((s_q, LANES), lambda b, h: (b, h)),
        out_shape=jax.ShapeDtypeStruct((n_batch * s_q, ATTN_WIDTH), BF16),
        scratch_shapes=[pltpu.VMEM((s_ctx + s_q, LANES), BF16), pltpu.VMEM((s_ctx + s_q, LANES), BF16)],
        compiler_params=_cparams(2),
        name="attn_ctx" if ctx is not None else "attn",
    )(*args)


def _conv_kernel(z_ref, w_ref, b_ref, g_ref, beta_ref, o_ref, pad_ref, *, s_len):
    zeros = jnp.zeros((2 * CONV_HALO - 8, CONV_WIDTH), F32)
    z = z_ref[...]
    for res in range(8):
        pad_ref[res, 0:CONV_HALO, :] = zeros[0:CONV_HALO]
        pad_ref[res, s_len + 8:s_len + 2 * CONV_HALO, :] = zeros
    for res in range(8):
        pad_ref[res, CONV_HALO - res:CONV_HALO - res + s_len, :] = z
    bias = b_ref[...]
    gam = g_ref[...]
    beta = beta_ref[...]
    off0 = CONV_HALO - CONV_KERNEL // 2

    def chunk(c, carry):
        r0 = pl.multiple_of(c * CONV_CHUNK, CONV_CHUNK)
        slabs = []
        for s0 in range(0, CONV_WIDTH, LANES):
            sl = slice(s0, s0 + LANES)
            part = jnp.broadcast_to(bias[:, sl], (CONV_CHUNK, LANES))
            for j in range(CONV_KERNEL):
                res = (off0 + j) % 8
                a8 = off0 + j - res
                part = part + pad_ref[res, pl.ds(r0 + a8, CONV_CHUNK), sl] * w_ref[j:j + 1, sl]
            slabs.append(part)
        acc = jnp.concatenate(slabs, axis=-1)
        mu = jnp.mean(acc, axis=-1, keepdims=True)
        d = acc - mu
        var = jnp.mean(d * d, axis=-1, keepdims=True)
        y = d * lax.rsqrt(var + EPS) * gam + beta
        o_ref[pl.ds(r0, CONV_CHUNK), :] = (y * jax.nn.sigmoid(y)).astype(BF16)
        return carry

    lax.fori_loop(0, s_len // CONV_CHUNK, chunk, 0)


def _conv_module(z, conv_w_l, conv_b_l, g_l, beta_l, n_batch, s_len, row_blk0):
    const2 = lambda b: (0, 0)
    return pl.pallas_call(
        functools.partial(_conv_kernel, s_len=s_len),
        grid=(n_batch,),
        in_specs=[
            pl.BlockSpec((s_len, CONV_WIDTH), lambda b: (row_blk0 + b, 0)),
            pl.BlockSpec((CONV_KERNEL, CONV_WIDTH), const2),
            pl.BlockSpec((1, CONV_WIDTH), const2),
            pl.BlockSpec((1, CONV_WIDTH), const2),
            pl.BlockSpec((1, CONV_WIDTH), const2),
        ],
        out_specs=pl.BlockSpec((s_len, CONV_WIDTH), lambda b: (b, 0)),
        out_shape=jax.ShapeDtypeStruct((n_batch * s_len, CONV_WIDTH), BF16),
        scratch_shapes=[pltpu.VMEM((8, s_len + 2 * CONV_HALO, CONV_WIDTH), F32)],
        compiler_params=_cparams(1),
        name="conv",
    )(z, conv_w_l, conv_b_l, g_l, beta_l)


def _route_rows(logits, seen, tri):
    rows = logits.shape[0]
    lane = lax.broadcasted_iota(jnp.int32, (rows, LANES), 1)
    lane_f = lane.astype(F32)
    work = logits
    sel_masks, top_vals, top_idx = [], [], []
    for _ in range(TOP_K):
        mx = jnp.max(work, axis=-1, keepdims=True)
        idx = jnp.min(jnp.where(work == mx, lane_f, float(LANES)), axis=-1, keepdims=True)
        sel = lane_f == idx
        sel_masks.append(sel)
        top_vals.append(mx)
        top_idx.append(idx)
        work = jnp.where(sel, NEG_BIG * 2.0, work)
    exps = [jnp.exp(v - top_vals[0]) for v in top_vals]
    denom = exps[0] + exps[1] + exps[2] + exps[3]
    onehot = jnp.zeros((rows, LANES), F32)
    for sel in sel_masks:
        onehot = onehot + jnp.where(sel, 1.0, 0.0)
    base = jnp.dot(tri, onehot.astype(BF16), preferred_element_type=F32) + seen
    te = jnp.zeros((rows, LANES), F32)
    gw = jnp.zeros((rows, LANES), F32)
    rk = jnp.zeros((rows, LANES), F32)
    for kk in range(TOP_K):
        r = jnp.sum(jnp.where(sel_masks[kk], base, 0.0), axis=-1, keepdims=True)
        col = lane == kk
        te = jnp.where(col, top_idx[kk], te)
        gw = jnp.where(col, exps[kk] / denom, gw)
        rk = jnp.where(col, r, rk)
    return te.astype(jnp.int32), gw, rk.astype(jnp.int32), jnp.sum(onehot, axis=0, keepdims=True)


def _outproj_kernel(xc_ref, xl_ref, op_ref, os_ref, zp_ref, zs_ref, gate_ref, mod_ref, wa_ref, wc_ref, wo_ref,
                    gffn_ref, wrh_ref, wrl_ref, br_ref, tri_ref,
                    x1_ref, h2_ref, te_ref, gw_ref, rank_ref, cnt_ref, carry_ref, *, n_ctx_tiles):
    i = pl.program_id(0)

    @pl.when(i == 0)
    def _():
        carry_ref[...] = jnp.zeros_like(carry_ref)

    m = mod_ref[0]
    is_ctx = i < n_ctx_tiles
    seen = carry_ref[...]
    for g0 in range(0, TMO, TM):
        rs = slice(g0, g0 + TM)
        o = jnp.where(is_ctx, op_ref[rs, :], os_ref[rs, :])
        zc = jnp.where(is_ctx, zp_ref[rs, :], zs_ref[rs, :])
        attn_d = jnp.dot(o, wa_ref[...], preferred_element_type=F32)
        conv_d = jnp.dot(zc, wc_ref[...], preferred_element_type=F32)
        y = (gate_ref[rs, 0:D_MODEL].astype(F32) * attn_d
             + gate_ref[rs, D_MODEL:2 * D_MODEL].astype(F32) * conv_d)
        mix = jnp.dot(y.astype(BF16), wo_ref[...], preferred_element_type=F32)
        x1 = jnp.where(is_ctx, xc_ref[rs, :], xl_ref[rs, :]) + m[2:3] * mix
        x1_ref[rs, :] = x1
        h2 = _rms_mod(x1, gffn_ref[...], m[3:4], m[4:5])
        h2_ref[rs, :] = h2
        h_hi = h2.astype(BF16)
        h_lo = (h2 - h_hi.astype(F32)).astype(BF16)
        logits = (jnp.dot(h_hi, wrh_ref[...], preferred_element_type=F32)
                  + (jnp.dot(h_lo, wrh_ref[...], preferred_element_type=F32)
                     + jnp.dot(h_hi, wrl_ref[...], preferred_element_type=F32))) + br_ref[...]
        te, gw, rk, cnt = _route_rows(logits, seen, tri_ref[...])
        te_ref[rs, :] = te
        gw_ref[rs, :] = gw
        rank_ref[rs, :] = rk
        seen = seen + cnt
    carry_ref[...] = seen
    cnt_ref[...] = jnp.broadcast_to(seen, cnt_ref.shape).astype(jnp.int32)


def _outproj(x, o_p, o_s, zc_p, zc_s, gates, mod_l, wa_bf, wc_bf, wo_bf, l, gffn, wr_hi, wr_lo, br_pad,
             tri, s_lat):
    n_ctx_tiles = o_p.shape[0] // TMO
    n_lat_tiles = o_s.shape[0] // TMO
    xs, n_tok, x_specs = _token_stream_specs(x, TMO, n_ctx_tiles)
    row = lambda i: (i, 0)
    ctxrow = lambda i: (jnp.minimum(i, n_ctx_tiles - 1), 0)
    latrow = lambda i: (jnp.clip(i - n_ctx_tiles, 0, n_lat_tiles - 1), 0)
    const2 = lambda i: (0, 0)
    lay3 = lambda i: (l, 0, 0)
    modrow = lambda i: (_mod_row_of_tile(i, n_ctx_tiles, s_lat // TMO), 0, 0)
    return pl.pallas_call(
        functools.partial(_outproj_kernel, n_ctx_tiles=n_ctx_tiles),
        grid=(n_tok // TMO,),
        in_specs=x_specs + [
            pl.BlockSpec((TMO, ATTN_WIDTH), ctxrow),
            pl.BlockSpec((TMO, ATTN_WIDTH), latrow),
            pl.BlockSpec((TMO, CONV_WIDTH), ctxrow),
            pl.BlockSpec((TMO, CONV_WIDTH), latrow),
            pl.BlockSpec((TMO, 2 * D_MODEL), row),
            pl.BlockSpec((1, 6, D_MODEL), modrow),
            pl.BlockSpec((None, ATTN_WIDTH, D_MODEL), lay3),
            pl.BlockSpec((None, CONV_WIDTH, D_MODEL), lay3),
            pl.BlockSpec((None, D_MODEL, D_MODEL), lay3),
            pl.BlockSpec((1, D_MODEL), const2),
            pl.BlockSpec((D_MODEL, LANES), const2),
            pl.BlockSpec((D_MODEL, LANES), const2),
            pl.BlockSpec((1, LANES), const2),
            pl.BlockSpec((TM, TM), const2),
        ],
        out_specs=[
            pl.BlockSpec((TMO, D_MODEL), row),
            pl.BlockSpec((TMO, D_MODEL), row),
            pl.BlockSpec((TMO, LANES), row),
            pl.BlockSpec((TMO, LANES), row),
            pl.BlockSpec((TMO, LANES), row),
            pl.BlockSpec((8, LANES), const2),
        ],
        out_shape=[
            jax.ShapeDtypeStruct((n_tok, D_MODEL), F32),
            jax.ShapeDtypeStruct((n_tok, D_MODEL), F32),
            jax.ShapeDtypeStruct((n_tok, LANES), jnp.int32),
            jax.ShapeDtypeStruct((n_tok, LANES), F32),
            jax.ShapeDtypeStruct((n_tok, LANES), jnp.int32),
            jax.ShapeDtypeStruct((8, LANES), jnp.int32),
        ],
        scratch_shapes=[pltpu.VMEM((1, LANES), F32)],
        compiler_params=_cparams(1),
        name="outproj_router",
    )(*xs, o_p, o_s, zc_p, zc_s, gates, mod_l, wa_bf, wc_bf, wo_bf, gffn, wr_hi, wr_lo, br_pad, tri)


def _invert_kernel(dest_ref, init_ref, inv_ref, sem):
    i = pl.program_id(0)

    @pl.when(i == 0)
    def _():
        cp = pltpu.make_async_copy(init_ref, inv_ref, sem)
        cp.start()
        cp.wait()

    base = i * INV_CHUNK

    def body(a, carry):
        inv_ref[dest_ref[a]] = base + a
        return carry

    lax.fori_loop(0, INV_CHUNK, body, 0, unroll=16)


def _invert(dest_flat, inv_init):
    return pl.pallas_call(
        _invert_kernel,
        grid=(dest_flat.shape[0] // INV_CHUNK,),
        in_specs=[
            pl.BlockSpec((INV_CHUNK,), lambda i: (i,), memory_space=pltpu.SMEM),
            pl.BlockSpec(memory_space=pl.ANY),
        ],
        out_specs=pl.BlockSpec(memory_space=pltpu.SMEM),
        out_shape=jax.ShapeDtypeStruct(inv_init.shape, jnp.int32),
        scratch_shapes=[pltpu.SemaphoreType.DMA(())],
        compiler_params=_cparams(1),
        name="moe_invert",
    )(dest_flat, inv_init)


def _expert_kernel(be_ref, nused_ref, par_ref, nxt_ref, inv_ref, h_ref, wgu_hbm, bgu_ref, wd_hbm, bd_ref,
                   y_ref, wgu_f32, wd_f32, wgu_bf, wd_bf, xb0, xb1, xb2, gsem, wsem, *, n_blk, layer):
    j = pl.program_id(0)
    n_used = nused_ref[0]
    xbuf = (xb0, xb1, xb2)

    def weight_copies(e, p):
        return (pltpu.make_async_copy(wgu_hbm.at[layer, e], wgu_f32.at[p], wsem.at[p]),
                pltpu.make_async_copy(wd_hbm.at[layer, e], wd_f32.at[p], wsem.at[p]))

    def gather_copy(r, slot):
        tok = inv_ref[r] >> 2
        return pltpu.make_async_copy(h_ref.at[pl.ds(tok, 1), :], xbuf[slot].at[pl.ds(r, 1), :],
                                     gsem.at[slot])

    def wait_gather(slot):
        pltpu.make_async_copy(h_ref.at[pl.ds(0, EBLK), :], xbuf[slot], gsem.at[slot]).wait()

    def rolled_gather(slot):
        def body(r, carry):
            gather_copy(r, slot).start()
            return carry
        lax.fori_loop(0, EBLK, body, 0)

    @pl.when(j == 0)
    def _():
        for cp in weight_copies(be_ref[0], 0):
            cp.start()
        rolled_gather(0)

    @pl.when(j == 1)
    def _():
        rolled_gather(1)

    blk = j - 2
    is_compute = jnp.logical_and(j >= 2, j <= n_used + 1)
    bclip = jnp.clip(blk, 0, n_blk - 1)
    e_cur = be_ref[bclip]
    e_prev = be_ref[jnp.clip(blk - 1, 0, n_blk - 1)]

    @pl.when(jnp.logical_and(is_compute, jnp.logical_or(blk == 0, e_cur != e_prev)))
    def _():
        p = par_ref[bclip]
        e_next = nxt_ref[bclip]
        for cp in weight_copies(e_cur, p):
            cp.wait()

        @pl.when(e_next >= 0)
        def _():
            for cp in weight_copies(e_next, 1 - p):
                cp.start()

        wgu_bf[...] = wgu_f32[p].astype(BF16)
        wd_bf[...] = wd_f32[p].astype(BF16)

    for slot in range(NBUF):
        nxt = (slot + 1) % NBUF
        prv = (slot + 2) % NBUF

        @pl.when(jnp.logical_and(is_compute, blk % NBUF == slot))
        def _():
            wait_gather(slot)
            for r in range(EBLK):
                gather_copy(r, prv).start()
            x = xbuf[slot][...].astype(BF16)
            gu = jnp.dot(x, wgu_bf[...], preferred_element_type=F32) + bgu_ref[...]
            g = jnp.minimum(gu[:, 0:D_EXPERT], SWIGLU_LIMIT)
            u = jnp.clip(gu[:, D_EXPERT:2 * D_EXPERT], -SWIGLU_LIMIT, SWIGLU_LIMIT)
            act = (u + 1.0) * (g * jax.nn.sigmoid(SWIGLU_ALPHA * g))
            y_ref[...] = jnp.dot(act.astype(BF16), wd_bf[...], preferred_element_type=F32) + bd_ref[...]

        @pl.when(jnp.logical_and(j == n_used + 2, (n_used - 1) % NBUF == slot))
        def _():
            wait_gather(nxt)
            wait_gather(prv)


def _experts(h2, inv, block_e, n_used, run_par, run_next, w_gate_up, b_gate_up, w_down, b_down, l):
    n_rows = inv.shape[0]
    n_blk = n_rows // EBLK
    cur = lambda j, nu: jnp.clip(j - 2, 0, nu[0] - 1)
    bblk = lambda j, be, nu, rp, rn: (l, be[cur(j, nu)], 0, 0)
    tile = pltpu.VMEM((EBLK, D_MODEL), F32)
    grid_spec = pltpu.PrefetchScalarGridSpec(
        num_scalar_prefetch=4,
        grid=(n_blk + 3,),
        in_specs=[
            pl.BlockSpec((EBLK,), lambda j, be, nu, rp, rn: (jnp.minimum(j, n_blk - 1),),
                         memory_space=pltpu.SMEM),
            pl.BlockSpec(memory_space=pl.ANY),
            pl.BlockSpec(memory_space=pl.ANY),
            pl.BlockSpec((None, None, 1, 2 * D_EXPERT), bblk),
            pl.BlockSpec(memory_space=pl.ANY),
            pl.BlockSpec((None, None, 1, D_MODEL), bblk),
        ],
        out_specs=pl.BlockSpec((EBLK, D_MODEL), lambda j, be, nu, rp, rn: (cur(j, nu), 0)),
        scratch_shapes=[
            pltpu.VMEM((2, D_MODEL, 2 * D_EXPERT), F32),
            pltpu.VMEM((2, D_EXPERT, D_MODEL), F32),
            pltpu.VMEM((D_MODEL, 2 * D_EXPERT), BF16),
            pltpu.VMEM((D_EXPERT, D_MODEL), BF16),
            tile, tile, tile,
            pltpu.SemaphoreType.DMA((NBUF,)),
            pltpu.SemaphoreType.DMA((2,)),
        ],
    )
    return pl.pallas_call(
        functools.partial(_expert_kernel, n_blk=n_blk, layer=l),
        grid_spec=grid_spec,
        out_shape=jax.ShapeDtypeStruct((n_rows, D_MODEL), F32),
        compiler_params=_cparams(1),
        name="moe_experts",
    )(block_e, n_used, run_par, run_next, inv, h2, w_gate_up,
      b_gate_up.reshape(DEPTH, N_EXPERTS, 1, 2 * D_EXPERT), w_down,
      b_down.reshape(DEPTH, N_EXPERTS, 1, D_MODEL))


def _combine_kernel(dcur_ref, dnxt_ref, x1_ref, mod_ref, gw_ref, yb_ref, *rest, n_ctx_tiles, split_out):
    i = pl.program_id(0)
    if split_out:
        octx_ref, olat_ref, buf0, buf1, sem = rest
    else:
        o_ref, buf0, buf1, sem = rest
    bufs = (buf0, buf1)

    def emit(x_new):
        if not split_out:
            o_ref[...] = x_new
            return

        @pl.when(i < n_ctx_tiles)
        def _():
            octx_ref[...] = x_new

        @pl.when(i >= n_ctx_tiles)
        def _():
            olat_ref[...] = x_new

    def row_copy(dest_ref, r, kk, slot):
        return pltpu.make_async_copy(yb_ref.at[pl.ds(dest_ref[r * TOP_K + kk], 1), :],
                                     bufs[slot].at[kk, pl.ds(r, 1), :], sem.at[slot])

    def wait_tile(slot):
        for kk in range(TOP_K):
            pltpu.make_async_copy(yb_ref.at[pl.ds(0, TM), :], bufs[slot].at[kk], sem.at[slot]).wait()

    @pl.when(i == 0)
    def _():
        def body(r, carry):
            for kk in range(TOP_K):
                row_copy(dcur_ref, r, kk, 0).start()
            return carry
        lax.fori_loop(0, TM, body, 0)

    for slot in range(2):
        @pl.when(i % 2 == slot)
        def _():
            @pl.when(i + 1 < pl.num_programs(0))
            def _():
                for r in range(TM):
                    for kk in range(TOP_K):
                        row_copy(dnxt_ref, r, kk, 1 - slot).start()

            wait_tile(slot)
            gw = gw_ref[...]
            y = gw[:, 0:1] * bufs[slot][0]
            for kk in range(1, TOP_K):
                y = y + gw[:, kk:kk + 1] * bufs[slot][kk]
            emit(x1_ref[...] + mod_ref[0][5:6] * y)


def _combine(x1, mod_l, gatew, dest_flat, yb, n_ctx_tiles, tiles_per_seq, split_out=False):
    n_tok = x1.shape[0]
    n_tiles = n_tok // TM
    row = lambda i: (i, 0)
    modrow = lambda i: (_mod_row_of_tile(i, n_ctx_tiles, tiles_per_seq), 0, 0)
    gbuf = pltpu.VMEM((TOP_K, TM, D_MODEL), F32)
    if split_out:
        out_specs = [pl.BlockSpec((TM, D_MODEL), lambda i: (jnp.minimum(i, n_ctx_tiles - 1), 0)),
                     pl.BlockSpec((TM, D_MODEL), lambda i: (jnp.maximum(i - n_ctx_tiles, 0), 0))]
        out_shape = [jax.ShapeDtypeStruct((n_ctx_tiles * TM, D_MODEL), F32),
                     jax.ShapeDtypeStruct((n_tok - n_ctx_tiles * TM, D_MODEL), F32)]
    else:
        out_specs = pl.BlockSpec((TM, D_MODEL), row)
        out_shape = jax.ShapeDtypeStruct((n_tok, D_MODEL), F32)
    return pl.pallas_call(
        functools.partial(_combine_kernel, n_ctx_tiles=n_ctx_tiles, split_out=split_out),
        grid=(n_tiles,),
        in_specs=[
            pl.BlockSpec((TM * TOP_K,), lambda i: (i,), memory_space=pltpu.SMEM),
            pl.BlockSpec((TM * TOP_K,), lambda i: (jnp.minimum(i + 1, n_tiles - 1),),
                         memory_space=pltpu.SMEM),
            pl.BlockSpec((TM, D_MODEL), row),
            pl.BlockSpec((1, 6, D_MODEL), modrow),
            pl.BlockSpec((TM, LANES), row),
            pl.BlockSpec(memory_space=pl.ANY),
        ],
        out_specs=out_specs,
        out_shape=out_shape,
        scratch_shapes=[gbuf, gbuf, pltpu.SemaphoreType.DMA((2,))],
        compiler_params=_cparams(1),
        name="moe_combine",
    )(dest_flat, dest_flat, x1, mod_l, gatew, yb)


def _rope_tables(s_len):
    rows = s_len // GRID_W
    row = jnp.repeat(jnp.arange(rows, dtype=F32), GRID_W)
    col = jnp.tile(jnp.arange(GRID_W, dtype=F32), rows)
    n_freq = HEAD_DIM // 4
    inv = ROPE_BASE ** (-jnp.arange(n_freq, dtype=F32) / n_freq)
    ang_r = row[:, None] * inv
    ang_c = col[:, None] * inv
    ang = jnp.concatenate([ang_r, ang_r, ang_c, ang_c], axis=1)
    sign = jnp.tile(jnp.concatenate([-jnp.ones((n_freq,), F32), jnp.ones((n_freq,), F32)]), 2)
    cos = jnp.tile(jnp.cos(ang), (1, 2))
    sin = jnp.tile(jnp.sin(ang) * sign, (1, 2))
    cos = jnp.concatenate([jnp.ones((TM, LANES), F32), cos], axis=0)
    sin = jnp.concatenate([jnp.zeros((TM, LANES), F32), sin], axis=0)
    return cos, sin


def kernel(x_prompt, x_sample, c, cache_k, cache_v, c_ctx, w_ada, b_ada, norm_mix, norm_ffn, w_in, q_norm, k_norm, lam_qk, sub_norm, w_attn_out, conv_w, conv_b, conv_norm_g, conv_norm_b, w_conv_out, w_out, w_router, b_router, w_gate_up, b_gate_up, w_down, b_down):
    b_p, s_p, _ = x_prompt.shape
    b_s, s_s, _ = x_sample.shape
    past = cache_k.shape[2]
    n_ctx = b_p * s_p
    n_tok = n_ctx + b_s * s_s
    n_ctx_tiles = n_ctx // TM
    tiles_per_seq = s_s // TM
    assert s_p == TM and s_s % TMO == 0 and n_ctx % s_s == 0 and s_s % GRID_W == 0
    assert (n_tok * TOP_K) % INV_CHUNK == 0 and (n_tok * TOP_K) % EBLK == 0

    x = (x_prompt.reshape(n_ctx, D_MODEL), x_sample.reshape(b_s * s_s, D_MODEL))
    n_cond = 8 * pl.cdiv(1 + b_s, 8)
    cond = jnp.zeros((n_cond, D_MODEL), F32).at[0].set(c_ctx).at[1:1 + b_s].set(c)
    mod = _ada_all_layers(cond, w_ada, b_ada).reshape(DEPTH, n_cond, 6, D_MODEL)

    w_in_bf = w_in.astype(BF16)
    wa_bf = w_attn_out.astype(BF16)
    wc_bf = w_conv_out.astype(BF16)
    wo_bf = w_out.astype(BF16)
    ck = cache_k.reshape(b_s, DEPTH, past, ATTN_WIDTH)
    cv = cache_v.reshape(b_s, DEPTH, past, ATTN_WIDTH)
    cos_t, sin_t = _rope_tables(s_s)
    grp = jnp.arange(ATTN_WIDTH, dtype=jnp.int32) // HEAD_DIM
    gsum = (grp[:, None] == grp[None, :]).astype(BF16)
    tri = (jnp.arange(TM)[:, None] > jnp.arange(TM)[None, :]).astype(BF16)
    n_blk = (n_tok * TOP_K) // EBLK + N_EXPERTS
    n_rows = n_blk * EBLK
    blk_row0 = jnp.arange(n_blk, dtype=jnp.int32) * EBLK
    inv_init = jnp.zeros((n_rows,), jnp.int32)

    new_k, new_v = [], []
    for l in range(DEPTH):
        lam_init = 0.8 - 0.6 * math.exp(-0.3 * l)
        q, k, v, zpre, gates = _inproj(
            x, mod[l], norm_mix[l].reshape(1, D_MODEL), w_in_bf, l,
            jnp.tile(q_norm[l], ATTN_WIDTH // HEAD_DIM).reshape(1, ATTN_WIDTH),
            jnp.tile(k_norm[l], ATTN_WIDTH // HEAD_DIM).reshape(1, ATTN_WIDTH),
            gsum, cos_t, sin_t, n_ctx_tiles, tiles_per_seq)
        new_k.append(k[:n_ctx].reshape(b_p, s_p, N_HEADS, 2 * HEAD_DIM))
        new_v.append(v[:n_ctx].reshape(b_p, s_p, N_HEADS, 2 * HEAD_DIM))
        sub_l = sub_norm[l].reshape(1, LANES)
        o_p = _attention(q, k, v, lam_qk[l], sub_l, lam_init, b_p, s_p, 0)
        o_s = _attention(q, k, v, lam_qk[l], sub_l, lam_init, b_s, s_s, n_ctx // s_s, ctx=(ck, cv, l))
        conv_args = (conv_w[l], conv_b[l].reshape(1, CONV_WIDTH), conv_norm_g[l].reshape(1, CONV_WIDTH),
                     conv_norm_b[l].reshape(1, CONV_WIDTH))
        zc_p = _conv_module(zpre, *conv_args, b_p, s_p, 0)
        zc_s = _conv_module(zpre, *conv_args, b_s, s_s, n_ctx // s_s)
        wr_pad = jnp.zeros((D_MODEL, LANES), F32).at[:, :N_EXPERTS].set(w_router[l])
        wr_hi = wr_pad.astype(BF16)
        wr_lo = (wr_pad - wr_hi.astype(F32)).astype(BF16)
        br_pad = jnp.full((1, LANES), NEG_BIG, F32).at[0, :N_EXPERTS].set(b_router[l])
        x1, h2, top_e, gatew, rank, counts = _outproj(
            x, o_p, o_s, zc_p, zc_s, gates, mod[l], wa_bf, wc_bf, wo_bf, l,
            norm_ffn[l].reshape(1, D_MODEL), wr_hi, wr_lo, br_pad, tri, s_s)

        cnt = counts[0, :N_EXPERTS]
        padded = (cnt + EBLK - 1) // EBLK * EBLK
        pend = jnp.cumsum(padded)
        pstart = pend - padded
        block_e = jnp.minimum(jnp.sum((pend[None, :] <= blk_row0[:, None]).astype(jnp.int32), axis=1),
                              N_EXPERTS - 1)
        n_used = (pend[-1] // EBLK).astype(jnp.int32).reshape(1)
        has = (cnt > 0).astype(jnp.int32)
        run_idx = jnp.cumsum(has) - has
        eid = jnp.arange(N_EXPERTS, dtype=jnp.int32)
        later = jnp.where((eid[None, :] > eid[:, None]) & (has[None, :] > 0), eid[None, :], N_EXPERTS)
        next_e = jnp.min(later, axis=1)
        next_e = jnp.where(next_e < N_EXPERTS, next_e, -1).astype(jnp.int32)
        run_par = (run_idx % 2).astype(jnp.int32)[block_e]
        run_next = next_e[block_e]

        dest = (pstart[top_e[:, :TOP_K]] + rank[:, :TOP_K]).reshape(n_tok * TOP_K)
        inv = _invert(dest, inv_init)
        yb = _experts(h2, inv, block_e, n_used, run_par, run_next, w_gate_up, b_gate_up, w_down, b_down, l)
        x = _combine(x1, mod[l], gatew, dest, yb, n_ctx_tiles, tiles_per_seq, split_out=(l == DEPTH - 1))

    y_prompt = x[0].reshape(b_p, s_p, D_MODEL)
    y_sample = x[1].reshape(b_s, s_s, D_MODEL)
    return (y_prompt, y_sample, jnp.stack(new_k, axis=1), jnp.stack(new_v, axis=1))
```

```python
import functools
import math

import jax
import jax.numpy as jnp
from jax import lax
from jax.experimental import pallas as pl
from jax.experimental.pallas import tpu as pltpu

F32 = jnp.float32
BF16 = jnp.bfloat16

D_MODEL = 1024
DEPTH = 4
GRID_W = 64
ATTN_WIDTH = 512
N_HEADS = 4
HEAD_DIM = 64
CONV_WIDTH = 512
CONV_KERNEL = 31
N_EXPERTS = 32
TOP_K = 4
D_EXPERT = 1024
SWIGLU_LIMIT = 7.0
SWIGLU_ALPHA = 1.702
ROPE_BASE = 10000.0
EPS = 1e-6

LANES = 128
TM = 256
TMO = 512
EBLK = 256
INV_CHUNK = 4096
NBUF = 3
ADA_TN = 1024
ATTN_Q_CHUNK = 256
CONV_CHUNK = 64
CONV_HALO = 16
NEG_BIG = -1e30
VMEM_LIMIT = 56 * 1024 * 1024


def _cparams(n_axes, vmem=None):
    return pltpu.CompilerParams(
        dimension_semantics=("arbitrary",) * n_axes,
        vmem_limit_bytes=VMEM_LIMIT if vmem is None else vmem)


def _token_stream_specs(x, tile, n_ctx_tiles):
    if isinstance(x, tuple):
        xs = x
        n_tok = x[0].shape[0] + x[1].shape[0]
        lat0 = n_ctx_tiles
    else:
        xs = (x, x)
        n_tok = x.shape[0]
        lat0 = 0
    n_tiles = n_tok // tile
    specs = [pl.BlockSpec((tile, D_MODEL), lambda i: (jnp.minimum(i, n_ctx_tiles - 1), 0)),
             pl.BlockSpec((tile, D_MODEL), lambda i: (jnp.clip(i, n_ctx_tiles, n_tiles - 1) - lat0, 0))]
    return xs, n_tok, specs


def _mod_row_of_tile(i, n_ctx_tiles, tiles_per_seq):
    return jnp.where(i < n_ctx_tiles, 0, 1 + (i - n_ctx_tiles) // tiles_per_seq)


def _ada_kernel(cond_ref, w_ref, b_ref, o_ref):
    cnd = cond_ref[...]
    act = cnd * jax.nn.sigmoid(cnd)
    o_ref[0] = jnp.dot(act.astype(BF16), w_ref[0].astype(BF16), preferred_element_type=F32) + b_ref[0]


def _ada_all_layers(cond, w_ada, b_ada):
    n_rows = cond.shape[0]
    n_col = w_ada.shape[2]
    return pl.pallas_call(
        _ada_kernel,
        grid=(DEPTH, n_col // ADA_TN),
        in_specs=[
            pl.BlockSpec((n_rows, D_MODEL), lambda l, j: (0, 0)),
            pl.BlockSpec((1, D_MODEL, ADA_TN), lambda l, j: (l, 0, j)),
            pl.BlockSpec((1, 1, ADA_TN), lambda l, j: (l, 0, j)),
        ],
        out_specs=pl.BlockSpec((1, n_rows, ADA_TN), lambda l, j: (l, 0, j)),
        out_shape=jax.ShapeDtypeStruct((DEPTH, n_rows, n_col), F32),
        compiler_params=_cparams(2),
        name="ada",
    )(cond, w_ada, b_ada.reshape(DEPTH, 1, n_col))


def _rms_mod(x, g, shift, scale):
    y = x * lax.rsqrt(jnp.mean(x * x, axis=-1, keepdims=True) + EPS) * g
    return y * (1.0 + scale) + shift


def _group_mean_sq(t, gsum):
    s = t * t
    hi = s.astype(BF16)
    lo = (s - hi.astype(F32)).astype(BF16)
    tot = jnp.dot(hi, gsum, preferred_element_type=F32) + jnp.dot(lo, gsum, preferred_element_type=F32)
    return tot * (1.0 / HEAD_DIM)


def _rope_slab(x, cos, sgn_sin, first_half):
    partner = jnp.where(first_half, pltpu.roll(x, LANES - 16, 1), pltpu.roll(x, 16, 1))
    return x * cos + partner * sgn_sin


def _inproj_kernel(xc_ref, xl_ref, mod_ref, gmix_ref, w_ref, qn_ref, kn_ref, gsum_ref, cos_ref, sin_ref,
                   q_ref, k_ref, v_ref, z_ref, gate_ref, *, n_ctx_tiles):
    x = jnp.where(pl.program_id(0) < n_ctx_tiles, xc_ref[...], xl_ref[...])
    m = mod_ref[0]
    h = _rms_mod(x, gmix_ref[...], m[0:1], m[1:2]).astype(BF16)
    aw = ATTN_WIDTH
    q = jnp.dot(h, w_ref[:, 0:aw], preferred_element_type=F32)
    k = jnp.dot(h, w_ref[:, aw:2 * aw], preferred_element_type=F32)
    gsum = gsum_ref[...]
    q = q * lax.rsqrt(_group_mean_sq(q, gsum) + EPS) * qn_ref[...]
    k = k * lax.rsqrt(_group_mean_sq(k, gsum) + EPS) * kn_ref[...]
    cos = cos_ref[...]
    sin = sin_ref[...]
    lane = lax.broadcasted_iota(jnp.int32, (TM, LANES), 1)
    first_half = (lane & 16) == 0
    for hd in range(N_HEADS):
        sl = slice(hd * LANES, (hd + 1) * LANES)
        q_ref[:, sl] = (_rope_slab(q[:, sl], cos, sin, first_half) * (HEAD_DIM ** -0.5)).astype(BF16)
        k_ref[:, sl] = _rope_slab(k[:, sl], cos, sin, first_half)
    v_ref[...] = jnp.dot(h, w_ref[:, 2 * aw:3 * aw], preferred_element_type=F32)
    c0 = 3 * aw
    a = jnp.dot(h, w_ref[:, c0:c0 + CONV_WIDTH], preferred_element_type=F32)
    g = jnp.dot(h, w_ref[:, c0 + CONV_WIDTH:c0 + 2 * CONV_WIDTH], preferred_element_type=F32)
    z_ref[...] = a * jax.nn.sigmoid(g)
    c1 = c0 + 2 * CONV_WIDTH
    gl = jnp.dot(h, w_ref[:, c1:c1 + 2 * D_MODEL], preferred_element_type=F32)
    gate_ref[...] = jax.nn.sigmoid(gl).astype(BF16)


def _inproj(x, mod_l, gmix, w_in_bf, l, qn, kn, gsum, cos_t, sin_t, n_ctx_tiles, tiles_per_seq):
    xs, n_tok, x_specs = _token_stream_specs(x, TM, n_ctx_tiles)
    in_w = w_in_bf.shape[2]
    row = lambda i: (i, 0)
    modrow = lambda i: (_mod_row_of_tile(i, n_ctx_tiles, tiles_per_seq), 0, 0)
    ropeblk = lambda i: (jnp.where(i < n_ctx_tiles, 0, 1 + (i - n_ctx_tiles) % tiles_per_seq), 0)
    const2 = lambda i: (0, 0)
    return pl.pallas_call(
        functools.partial(_inproj_kernel, n_ctx_tiles=n_ctx_tiles),
        grid=(n_tok // TM,),
        in_specs=x_specs + [
            pl.BlockSpec((1, 6, D_MODEL), modrow),
            pl.BlockSpec((1, D_MODEL), const2),
            pl.BlockSpec((None, D_MODEL, in_w), lambda i: (l, 0, 0)),
            pl.BlockSpec((1, ATTN_WIDTH), const2),
            pl.BlockSpec((1, ATTN_WIDTH), const2),
            pl.BlockSpec((ATTN_WIDTH, ATTN_WIDTH), const2),
            pl.BlockSpec((TM, LANES), ropeblk),
            pl.BlockSpec((TM, LANES), ropeblk),
        ],
        out_specs=[
            pl.BlockSpec((TM, ATTN_WIDTH), row),
            pl.BlockSpec((TM, ATTN_WIDTH), row),
            pl.BlockSpec((TM, ATTN_WIDTH), row),
            pl.BlockSpec((TM, CONV_WIDTH), row),
            pl.BlockSpec((TM, 2 * D_MODEL), row),
        ],
        out_shape=[
            jax.ShapeDtypeStruct((n_tok, ATTN_WIDTH), BF16),
            jax.ShapeDtypeStruct((n_tok, ATTN_WIDTH), F32),
            jax.ShapeDtypeStruct((n_tok, ATTN_WIDTH), F32),
            jax.ShapeDtypeStruct((n_tok, CONV_WIDTH), F32),
            jax.ShapeDtypeStruct((n_tok, 2 * D_MODEL), BF16),
        ],
        compiler_params=_cparams(1),
        name="inproj",
    )(*xs, mod_l, gmix, w_in_bf, qn, kn, gsum, cos_t, sin_t)


def _attn_kernel(*refs, lam_init, s_q, s_ctx, q_chunk):
    if s_ctx:
        q_ref, k_ref, v_ref, ck_ref, cv_ref, lam_ref, sub_ref, o_ref, kc_ref, vc_ref = refs
        kc_ref[0:s_ctx, :] = ck_ref[...].astype(BF16)
        vc_ref[0:s_ctx, :] = cv_ref[...].astype(BF16)
    else:
        q_ref, k_ref, v_ref, lam_ref, sub_ref, o_ref, kc_ref, vc_ref = refs
    kc_ref[s_ctx:s_ctx + s_q, :] = k_ref[...].astype(BF16)
    vc_ref[s_ctx:s_ctx + s_q, :] = v_ref[...].astype(BF16)
    lq = lam_ref[...]
    lam = (jnp.exp(jnp.sum(lq[0:1] * lq[1:2], axis=-1, keepdims=True))
           - jnp.exp(jnp.sum(lq[2:3] * lq[3:4], axis=-1, keepdims=True)) + lam_init)
    lane = lax.broadcasted_iota(jnp.int32, (q_chunk, LANES), 1)
    map0 = lane < HEAD_DIM
    sub = sub_ref[...]
    nt = (((1,), (1,)), ((), ()))

    def chunk(c, carry):
        r0 = pl.multiple_of(c * q_chunk, q_chunk)
        q = q_ref[pl.ds(r0, q_chunk), :]
        kc = kc_ref[...]
        zero = jnp.zeros_like(q)
        exps, inv_sums = [], []
        for mp in range(2):
            qm = jnp.where(map0, q, zero) if mp == 0 else jnp.where(map0, zero, q)
            s = lax.dot_general(qm, kc, nt, preferred_element_type=F32)
            e = jnp.exp(s - jnp.max(s, axis=-1, keepdims=True))
            inv_sums.append(1.0 / jnp.sum(e, axis=-1, keepdims=True))
            exps.append(e.astype(BF16))
        pv = jnp.dot(jnp.concatenate(exps, axis=0), vc_ref[...], preferred_element_type=F32)
        o = pv[0:q_chunk] * inv_sums[0] - lam * (pv[q_chunk:2 * q_chunk] * inv_sums[1])
        o = o * lax.rsqrt(jnp.mean(o * o, axis=-1, keepdims=True) + EPS) * sub * (1.0 - lam_init)
        o_ref[pl.ds(r0, q_chunk), :] = o.astype(BF16)
        return carry

    lax.fori_loop(0, s_q // q_chunk, chunk, 0, unroll=min(2, s_q // q_chunk))


def _attention(q, k, v, lam_qk_l, sub_l, lam_init, n_batch, s_q, row_blk0, ctx=None):
    s_ctx = 0 if ctx is None else ctx[0].shape[2]
    own = pl.BlockSpec((s_q, LANES), lambda b, h: (row_blk0 + b, h))
    in_specs = [own, own, own]
    args = [q, k, v]
    if ctx is not None:
        ck, cv, l = ctx
        cspec = pl.BlockSpec((None, None, s_ctx, LANES), lambda b, h: (b, l, 0, h))
        in_specs += [cspec, cspec]
        args += [ck, cv]
    in_specs += [pl.BlockSpec((4, HEAD_DIM), lambda b, h: (0, 0)),
                 pl.BlockSpec((1, LANES), lambda b, h: (0, 0))]
    args += [lam_qk_l, sub_l]
    return pl.pallas_call(
        functools.partial(_attn_kernel, lam_init=lam_init, s_q=s_q, s_ctx=s_ctx, q_chunk=min(s_q, ATTN_Q_CHUNK)),
        grid=(n_batch, N_HEADS),
        in_specs=in_specs,
        out_specs=pl.BlockSpec((s_q, LANES), lambda b, h: (b, h)),
        out_shape=jax.ShapeDtypeStruct((n_batch * s_q, ATTN_WIDTH), BF16),
        scratch_shapes=[pltpu.VMEM((s_ctx + s_q, LANES), BF16), pltpu.VMEM((s_ctx + s_q, LANES), BF16)],
        compiler_params=_cparams(2),
        name="attn_ctx" if ctx is not None else "attn",
    )(*args)


def _conv_kernel(z_ref, w_ref, b_ref, g_ref, beta_ref, o_ref, pad_ref, *, s_len):
    zeros = jnp.zeros((2 * CONV_HALO - 8, CONV_WIDTH), F32)
    z = z_ref[...]
    for res in range(8):
        pad_ref[res, 0:CONV_HALO, :] = zeros[0:CONV_HALO]
        pad_ref[res, s_len + 8:s_len + 2 * CONV_HALO, :] = zeros
    for res in range(8):
        pad_ref[res, CONV_HALO - res:CONV_HALO - res + s_len, :] = z
    bias = b_ref[...]
    gam = g_ref[...]
    beta = beta_ref[...]
    off0 = CONV_HALO - CONV_KERNEL // 2

    def chunk(c, carry):
        r0 = pl.multiple_of(c * CONV_CHUNK, CONV_CHUNK)
        slabs = []
        for s0 in range(0, CONV_WIDTH, LANES):
            sl = slice(s0, s0 + LANES)
            part = jnp.broadcast_to(bias[:, sl], (CONV_CHUNK, LANES))
            for j in range(CONV_KERNEL):
                res = (off0 + j) % 8
                a8 = off0 + j - res
                part = part + pad_ref[res, pl.ds(r0 + a8, CONV_CHUNK), sl] * w_ref[j:j + 1, sl]
            slabs.append(part)
        acc = jnp.concatenate(slabs, axis=-1)
        mu = jnp.mean(acc, axis=-1, keepdims=True)
        d = acc - mu
        var = jnp.mean(d * d, axis=-1, keepdims=True)
        y = d * lax.rsqrt(var + EPS) * gam + beta
        o_ref[pl.ds(r0, CONV_CHUNK), :] = (y * jax.nn.sigmoid(y)).astype(BF16)
        return carry

    lax.fori_loop(0, s_len // CONV_CHUNK, chunk, 0)


def _conv_module(z, conv_w_l, conv_b_l, g_l, beta_l, n_batch, s_len, row_blk0):
    const2 = lambda b: (0, 0)
    return pl.pallas_call(
        functools.partial(_conv_kernel, s_len=s_len),
        grid=(n_batch,),
        in_specs=[
            pl.BlockSpec((s_len, CONV_WIDTH), lambda b: (row_blk0 + b, 0)),
            pl.BlockSpec((CONV_KERNEL, CONV_WIDTH), const2),
            pl.BlockSpec((1, CONV_WIDTH), const2),
            pl.BlockSpec((1, CONV_WIDTH), const2),
            pl.BlockSpec((1, CONV_WIDTH), const2),
        ],
        out_specs=pl.BlockSpec((s_len, CONV_WIDTH), lambda b: (b, 0)),
        out_shape=jax.ShapeDtypeStruct((n_batch * s_len, CONV_WIDTH), BF16),
        scratch_shapes=[pltpu.VMEM((8, s_len + 2 * CONV_HALO, CONV_WIDTH), F32)],
        compiler_params=_cparams(1),
        name="conv",
    )(z, conv_w_l, conv_b_l, g_l, beta_l)


def _route_rows(logits, seen, tri):
    rows = logits.shape[0]
    lane = lax.broadcasted_iota(jnp.int32, (rows, LANES), 1)
    lane_f = lane.astype(F32)
    work = logits
    sel_masks, top_vals, top_idx = [], [], []
    for _ in range(TOP_K):
        mx = jnp.max(work, axis=-1, keepdims=True)
        idx = jnp.min(jnp.where(work == mx, lane_f, float(LANES)), axis=-1, keepdims=True)
        sel = lane_f == idx
        sel_masks.append(sel)
        top_vals.append(mx)
        top_idx.append(idx)
        work = jnp.where(sel, NEG_BIG * 2.0, work)
    exps = [jnp.exp(v - top_vals[0]) for v in top_vals]
    denom = exps[0] + exps[1] + exps[2] + exps[3]
    onehot = jnp.zeros((rows, LANES), F32)
    for sel in sel_masks:
        onehot = onehot + jnp.where(sel, 1.0, 0.0)
    base = jnp.dot(tri, onehot.astype(BF16), preferred_element_type=F32) + seen
    te = jnp.zeros((rows, LANES), F32)
    gw = jnp.zeros((rows, LANES), F32)
    rk = jnp.zeros((rows, LANES), F32)
    for kk in range(TOP_K):
        r = jnp.sum(jnp.where(sel_masks[kk], base, 0.0), axis=-1, keepdims=True)
        col = lane == kk
        te = jnp.where(col, top_idx[kk], te)
        gw = jnp.where(col, exps[kk] / denom, gw)
        rk = jnp.where(col, r, rk)
    return te.astype(jnp.int32), gw, rk.astype(jnp.int32), jnp.sum(onehot, axis=0, keepdims=True)


def _outproj_kernel(xc_ref, xl_ref, op_ref, os_ref, zp_ref, zs_ref, gate_ref, mod_ref, wa_ref, wc_ref, wo_ref,
                    gffn_ref, wrh_ref, wrl_ref, br_ref, tri_ref,
                    x1_ref, h2_ref, te_ref, gw_ref, rank_ref, cnt_ref, carry_ref, *, n_ctx_tiles):
    i = pl.program_id(0)

    @pl.when(i == 0)
    def _():
        carry_ref[...] = jnp.zeros_like(carry_ref)

    m = mod_ref[0]
    is_ctx = i < n_ctx_tiles
    seen = carry_ref[...]
    for g0 in range(0, TMO, TM):
        rs = slice(g0, g0 + TM)
        o = jnp.where(is_ctx, op_ref[rs, :], os_ref[rs, :])
        zc = jnp.where(is_ctx, zp_ref[rs, :], zs_ref[rs, :])
        attn_d = jnp.dot(o, wa_ref[...], preferred_element_type=F32)
        conv_d = jnp.dot(zc, wc_ref[...], preferred_element_type=F32)
        y = (gate_ref[rs, 0:D_MODEL].astype(F32) * attn_d
             + gate_ref[rs, D_MODEL:2 * D_MODEL].astype(F32) * conv_d)
        mix = jnp.dot(y.astype(BF16), wo_ref[...], preferred_element_type=F32)
        x1 = jnp.where(is_ctx, xc_ref[rs, :], xl_ref[rs, :]) + m[2:3] * mix
        x1_ref[rs, :] = x1
        h2 = _rms_mod(x1, gffn_ref[...], m[3:4], m[4:5])
        h2_ref[rs, :] = h2
        h_hi = h2.astype(BF16)
        h_lo = (h2 - h_hi.astype(F32)).astype(BF16)
        logits = (jnp.dot(h_hi, wrh_ref[...], preferred_element_type=F32)
                  + (jnp.dot(h_lo, wrh_ref[...], preferred_element_type=F32)
                     + jnp.dot(h_hi, wrl_ref[...], preferred_element_type=F32))) + br_ref[...]
        te, gw, rk, cnt = _route_rows(logits, seen, tri_ref[...])
        te_ref[rs, :] = te
        gw_ref[rs, :] = gw
        rank_ref[rs, :] = rk
        seen = seen + cnt
    carry_ref[...] = seen
    cnt_ref[...] = jnp.broadcast_to(seen, cnt_ref.shape).astype(jnp.int32)


def _outproj(x, o_p, o_s, zc_p, zc_s, gates, mod_l, wa_bf, wc_bf, wo_bf, l, gffn, wr_hi, wr_lo, br_pad,
             tri, s_lat):
    n_ctx_tiles = o_p.shape[0] // TMO
    n_lat_tiles = o_s.shape[0] // TMO
    xs, n_tok, x_specs = _token_stream_specs(x, TMO, n_ctx_tiles)
    row = lambda i: (i, 0)
    ctxrow = lambda i: (jnp.minimum(i, n_ctx_tiles - 1), 0)
    latrow = lambda i: (jnp.clip(i - n_ctx_tiles, 0, n_lat_tiles - 1), 0)
    const2 = lambda i: (0, 0)
    lay3 = lambda i: (l, 0, 0)
    modrow = lambda i: (_mod_row_of_tile(i, n_ctx_tiles, s_lat // TMO), 0, 0)
    return pl.pallas_call(
        functools.partial(_outproj_kernel, n_ctx_tiles=n_ctx_tiles),
        grid=(n_tok // TMO,),
        in_specs=x_specs + [
            pl.BlockSpec((TMO, ATTN_WIDTH), ctxrow),
            pl.BlockSpec((TMO, ATTN_WIDTH), latrow),
            pl.BlockSpec((TMO, CONV_WIDTH), ctxrow),
            pl.BlockSpec((TMO, CONV_WIDTH), latrow),
            pl.BlockSpec((TMO, 2 * D_MODEL), row),
            pl.BlockSpec((1, 6, D_MODEL), modrow),
            pl.BlockSpec((None, ATTN_WIDTH, D_MODEL), lay3),
            pl.BlockSpec((None, CONV_WIDTH, D_MODEL), lay3),
            pl.BlockSpec((None, D_MODEL, D_MODEL), lay3),
            pl.BlockSpec((1, D_MODEL), const2),
            pl.BlockSpec((D_MODEL, LANES), const2),
            pl.BlockSpec((D_MODEL, LANES), const2),
            pl.BlockSpec((1, LANES), const2),
            pl.BlockSpec((TM, TM), const2),
        ],
        out_specs=[
            pl.BlockSpec((TMO, D_MODEL), row),
            pl.BlockSpec((TMO, D_MODEL), row),
            pl.BlockSpec((TMO, LANES), row),
            pl.BlockSpec((TMO, LANES), row),
            pl.BlockSpec((TMO, LANES), row),
            pl.BlockSpec((8, LANES), const2),
        ],
        out_shape=[
            jax.ShapeDtypeStruct((n_tok, D_MODEL), F32),
            jax.ShapeDtypeStruct((n_tok, D_MODEL), F32),
            jax.ShapeDtypeStruct((n_tok, LANES), jnp.int32),
            jax.ShapeDtypeStruct((n_tok, LANES), F32),
            jax.ShapeDtypeStruct((n_tok, LANES), jnp.int32),
            jax.ShapeDtypeStruct((8, LANES), jnp.int32),
        ],
        scratch_shapes=[pltpu.VMEM((1, LANES), F32)],
        compiler_params=_cparams(1),
        name="outproj_router",
    )(*xs, o_p, o_s, zc_p, zc_s, gates, mod_l, wa_bf, wc_bf, wo_bf, gffn, wr_hi, wr_lo, br_pad, tri)


def _invert_kernel(dest_ref, init_ref, inv_ref, sem):
    i = pl.program_id(0)

    @pl.when(i == 0)
    def _():
        cp = pltpu.make_async_copy(init_ref, inv_ref, sem)
        cp.start()
        cp.wait()

    base = i * INV_CHUNK

    def body(a, carry):
        inv_ref[dest_ref[a]] = base + a
        return carry

    lax.fori_loop(0, INV_CHUNK, body, 0, unroll=16)


def _invert(dest_flat, inv_init):
    return pl.pallas_call(
        _invert_kernel,
        grid=(dest_flat.shape[0] // INV_CHUNK,),
        in_specs=[
            pl.BlockSpec((INV_CHUNK,), lambda i: (i,), memory_space=pltpu.SMEM),
            pl.BlockSpec(memory_space=pl.ANY),
        ],
        out_specs=pl.BlockSpec(memory_space=pltpu.SMEM),
        out_shape=jax.ShapeDtypeStruct(inv_init.shape, jnp.int32),
        scratch_shapes=[pltpu.SemaphoreType.DMA(())],
        compiler_params=_cparams(1),
        name="moe_invert",
    )(dest_flat, inv_init)


def _expert_kernel(be_ref, nused_ref, par_ref, nxt_ref, inv_ref, h_ref, wgu_hbm, bgu_ref, wd_hbm, bd_ref,
                   y_ref, wgu_f32, wd_f32, wgu_bf, wd_bf, xb0, xb1, xb2, gsem, wsem, *, n_blk, layer):
    j = pl.program_id(0)
    n_used = nused_ref[0]
    xbuf = (xb0, xb1, xb2)

    def weight_copies(e, p):
        return (pltpu.make_async_copy(wgu_hbm.at[layer, e], wgu_f32.at[p], wsem.at[p]),
                pltpu.make_async_copy(wd_hbm.at[layer, e], wd_f32.at[p], wsem.at[p]))

    def gather_copy(r, slot):
        tok = inv_ref[r] >> 2
        return pltpu.make_async_copy(h_ref.at[pl.ds(tok, 1), :], xbuf[slot].at[pl.ds(r, 1), :],
                                     gsem.at[slot])

    def wait_gather(slot):
        pltpu.make_async_copy(h_ref.at[pl.ds(0, EBLK), :], xbuf[slot], gsem.at[slot]).wait()

    def rolled_gather(slot):
        def body(r, carry):
            gather_copy(r, slot).start()
            return carry
        lax.fori_loop(0, EBLK, body, 0)

    @pl.when(j == 0)
    def _():
        for cp in weight_copies(be_ref[0], 0):
            cp.start()
        rolled_gather(0)

    @pl.when(j == 1)
    def _():
        rolled_gather(1)

    blk = j - 2
    is_compute = jnp.logical_and(j >= 2, j <= n_used + 1)
    bclip = jnp.clip(blk, 0, n_blk - 1)
    e_cur = be_ref[bclip]
    e_prev = be_ref[jnp.clip(blk - 1, 0, n_blk - 1)]

    @pl.when(jnp.logical_and(is_compute, jnp.logical_or(blk == 0, e_cur != e_prev)))
    def _():
        p = par_ref[bclip]
        e_next = nxt_ref[bclip]
        for cp in weight_copies(e_cur, p):
            cp.wait()

        @pl.when(e_next >= 0)
        def _():
            for cp in weight_copies(e_next, 1 - p):
                cp.start()

        wgu_bf[...] = wgu_f32[p].astype(BF16)
        wd_bf[...] = wd_f32[p].astype(BF16)

    for slot in range(NBUF):
        nxt = (slot + 1) % NBUF
        prv = (slot + 2) % NBUF

        @pl.when(jnp.logical_and(is_compute, blk % NBUF == slot))
        def _():
            wait_gather(slot)
            for r in range(EBLK):
                gather_copy(r, prv).start()
            x = xbuf[slot][...].astype(BF16)
            gu = jnp.dot(x, wgu_bf[...], preferred_element_type=F32) + bgu_ref[...]
            g = jnp.minimum(gu[:, 0:D_EXPERT], SWIGLU_LIMIT)
            u = jnp.clip(gu[:, D_EXPERT:2 * D_EXPERT], -SWIGLU_LIMIT, SWIGLU_LIMIT)
            act = (u + 1.0) * (g * jax.nn.sigmoid(SWIGLU_ALPHA * g))
            y_ref[...] = jnp.dot(act.astype(BF16), wd_bf[...], preferred_element_type=F32) + bd_ref[...]

        @pl.when(jnp.logical_and(j == n_used + 2, (n_used - 1) % NBUF == slot))
        def _():
            wait_gather(nxt)
            wait_gather(prv)


def _experts(h2, inv, block_e, n_used, run_par, run_next, w_gate_up, b_gate_up, w_down, b_down, l):
    n_rows = inv.shape[0]
    n_blk = n_rows // EBLK
    cur = lambda j, nu: jnp.clip(j - 2, 0, nu[0] - 1)
    bblk = lambda j, be, nu, rp, rn: (l, be[cur(j, nu)], 0, 0)
    tile = pltpu.VMEM((EBLK, D_MODEL), F32)
    grid_spec = pltpu.PrefetchScalarGridSpec(
        num_scalar_prefetch=4,
        grid=(n_blk + 3,),
        in_specs=[
            pl.BlockSpec((EBLK,), lambda j, be, nu, rp, rn: (jnp.minimum(j, n_blk - 1),),
                         memory_space=pltpu.SMEM),
            pl.BlockSpec(memory_space=pl.ANY),
            pl.BlockSpec(memory_space=pl.ANY),
            pl.BlockSpec((None, None, 1, 2 * D_EXPERT), bblk),
            pl.BlockSpec(memory_space=pl.ANY),
            pl.BlockSpec((None, None, 1, D_MODEL), bblk),
        ],
        out_specs=pl.BlockSpec((EBLK, D_MODEL), lambda j, be, nu, rp, rn: (cur(j, nu), 0)),
        scratch_shapes=[
            pltpu.VMEM((2, D_MODEL, 2 * D_EXPERT), F32),
            pltpu.VMEM((2, D_EXPERT, D_MODEL), F32),
            pltpu.VMEM((D_MODEL, 2 * D_EXPERT), BF16),
            pltpu.VMEM((D_EXPERT, D_MODEL), BF16),
            tile, tile, tile,
            pltpu.SemaphoreType.DMA((NBUF,)),
            pltpu.SemaphoreType.DMA((2,)),
        ],
    )
    return pl.pallas_call(
        functools.partial(_expert_kernel, n_blk=n_blk, layer=l),
        grid_spec=grid_spec,
        out_shape=jax.ShapeDtypeStruct((n_rows, D_MODEL), F32),
        compiler_params=_cparams(1),
        name="moe_experts",
    )(block_e, n_used, run_par, run_next, inv, h2, w_gate_up,
      b_gate_up.reshape(DEPTH, N_EXPERTS, 1, 2 * D_EXPERT), w_down,
      b_down.reshape(DEPTH, N_EXPERTS, 1, D_MODEL))


def _combine_kernel(dcur_ref, dnxt_ref, x1_ref, mod_ref, gw_ref, yb_ref, *rest, n_ctx_tiles, split_out):
    i = pl.program_id(0)
    if split_out:
        octx_ref, olat_ref, buf0, buf1, sem = rest
    else:
        o_ref, buf0, buf1, sem = rest
    bufs = (buf0, buf1)

    def emit(x_new):
        if not split_out:
            o_ref[...] = x_new
            return

        @pl.when(i < n_ctx_tiles)
        def _():
            octx_ref[...] = x_new

        @pl.when(i >= n_ctx_tiles)
        def _():
            olat_ref[...] = x_new

    def row_copy(dest_ref, r, kk, slot):
        return pltpu.make_async_copy(yb_ref.at[pl.ds(dest_ref[r * TOP_K + kk], 1), :],
                                     bufs[slot].at[kk, pl.ds(r, 1), :], sem.at[slot])

    def wait_tile(slot):
        for kk in range(TOP_K):
            pltpu.make_async_copy(yb_ref.at[pl.ds(0, TM), :], bufs[slot].at[kk], sem.at[slot]).wait()

    @pl.when(i == 0)
    def _():
        def body(r, carry):
            for kk in range(TOP_K):
                row_copy(dcur_ref, r, kk, 0).start(priority=kk % 2)
            return carry
        lax.fori_loop(0, TM, body, 0)

    for slot in range(2):
        @pl.when(i % 2 == slot)
        def _():
            @pl.when(i + 1 < pl.num_programs(0))
            def _():
                for r in range(TM):
                    for kk in range(TOP_K):
                        row_copy(dnxt_ref, r, kk, 1 - slot).start(priority=kk % 2)

            wait_tile(slot)
            gw = gw_ref[...]
            y = gw[:, 0:1] * bufs[slot][0]
            for kk in range(1, TOP_K):
                y = y + gw[:, kk:kk + 1] * bufs[slot][kk]
            emit(x1_ref[...] + mod_ref[0][5:6] * y)


def _combine(x1, mod_l, gatew, dest_flat, yb, n_ctx_tiles, tiles_per_seq, split_out=False):
    n_tok = x1.shape[0]
    n_tiles = n_tok // TM
    row = lambda i: (i, 0)
    modrow = lambda i: (_mod_row_of_tile(i, n_ctx_tiles, tiles_per_seq), 0, 0)
    gbuf = pltpu.VMEM((TOP_K, TM, D_MODEL), F32)
    if split_out:
        out_specs = [pl.BlockSpec((TM, D_MODEL), lambda i: (jnp.minimum(i, n_ctx_tiles - 1), 0)),
                     pl.BlockSpec((TM, D_MODEL), lambda i: (jnp.maximum(i - n_ctx_tiles, 0), 0))]
        out_shape = [jax.ShapeDtypeStruct((n_ctx_tiles * TM, D_MODEL), F32),
                     jax.ShapeDtypeStruct((n_tok - n_ctx_tiles * TM, D_MODEL), F32)]
    else:
        out_specs = pl.BlockSpec((TM, D_MODEL), row)
        out_shape = jax.ShapeDtypeStruct((n_tok, D_MODEL), F32)
    return pl.pallas_call(
        functools.partial(_combine_kernel, n_ctx_tiles=n_ctx_tiles, split_out=split_out),
        grid=(n_tiles,),
        in_specs=[
            pl.BlockSpec((TM * TOP_K,), lambda i: (i,), memory_space=pltpu.SMEM),
            pl.BlockSpec((TM * TOP_K,), lambda i: (jnp.minimum(i + 1, n_tiles - 1),),
                         memory_space=pltpu.SMEM),
            pl.BlockSpec((TM, D_MODEL), row),
            pl.BlockSpec((1, 6, D_MODEL), modrow),
            pl.BlockSpec((TM, LANES), row),
            pl.BlockSpec(memory_space=pl.ANY),
        ],
        out_specs=out_specs,
        out_shape=out_shape,
        scratch_shapes=[gbuf, gbuf, pltpu.SemaphoreType.DMA((2,))],
        compiler_params=_cparams(1),
        name="moe_combine",
    )(dest_flat, dest_flat, x1, mod_l, gatew, yb)


def _rope_tables(s_len):
    rows = s_len // GRID_W
    row = jnp.repeat(jnp.arange(rows, dtype=F32), GRID_W)
    col = jnp.tile(jnp.arange(GRID_W, dtype=F32), rows)
    n_freq = HEAD_DIM // 4
    inv = ROPE_BASE ** (-jnp.arange(n_freq, dtype=F32) / n_freq)
    ang_r = row[:, None] * inv
    ang_c = col[:, None] * inv
    ang = jnp.concatenate([ang_r, ang_r, ang_c, ang_c], axis=1)
    sign = jnp.tile(jnp.concatenate([-jnp.ones((n_freq,), F32), jnp.ones((n_freq,), F32)]), 2)
    cos = jnp.tile(jnp.cos(ang), (1, 2))
    sin = jnp.tile(jnp.sin(ang) * sign, (1, 2))
    cos = jnp.concatenate([jnp.ones((TM, LANES), F32), cos], axis=0)
    sin = jnp.concatenate([jnp.zeros((TM, LANES), F32), sin], axis=0)
    return cos, sin


def kernel(x_prompt, x_sample, c, cache_k, cache_v, c_ctx, w_ada, b_ada, norm_mix, norm_ffn, w_in, q_norm, k_norm, lam_qk, sub_norm, w_attn_out, conv_w, conv_b, conv_norm_g, conv_norm_b, w_conv_out, w_out, w_router, b_router, w_gate_up, b_gate_up, w_down, b_down):
    b_p, s_p, _ = x_prompt.shape
    b_s, s_s, _ = x_sample.shape
    past = cache_k.shape[2]
    n_ctx = b_p * s_p
    n_tok = n_ctx + b_s * s_s
    n_ctx_tiles = n_ctx // TM
    tiles_per_seq = s_s // TM
    assert s_p == TM and s_s % TMO == 0 and n_ctx % s_s == 0 and s_s % GRID_W == 0
    assert (n_tok * TOP_K) % INV_CHUNK == 0 and (n_tok * TOP_K) % EBLK == 0

    x = (x_prompt.reshape(n_ctx, D_MODEL), x_sample.reshape(b_s * s_s, D_MODEL))
    n_cond = 8 * pl.cdiv(1 + b_s, 8)
    cond = jnp.zeros((n_cond, D_MODEL), F32).at[0].set(c_ctx).at[1:1 + b_s].set(c)
    mod = _ada_all_layers(cond, w_ada, b_ada).reshape(DEPTH, n_cond, 6, D_MODEL)

    w_in_bf = w_in.astype(BF16)
    wa_bf = w_attn_out.astype(BF16)
    wc_bf = w_conv_out.astype(BF16)
    wo_bf = w_out.astype(BF16)
    ck = cache_k.reshape(b_s, DEPTH, past, ATTN_WIDTH)
    cv = cache_v.reshape(b_s, DEPTH, past, ATTN_WIDTH)
    cos_t, sin_t = _rope_tables(s_s)
    grp = jnp.arange(ATTN_WIDTH, dtype=jnp.int32) // HEAD_DIM
    gsum = (grp[:, None] == grp[None, :]).astype(BF16)
    tri = (jnp.arange(TM)[:, None] > jnp.arange(TM)[None, :]).astype(BF16)
    n_blk = (n_tok * TOP_K) // EBLK + N_EXPERTS
    n_rows = n_blk * EBLK
    blk_row0 = jnp.arange(n_blk, dtype=jnp.int32) * EBLK
    inv_init = jnp.zeros((n_rows,), jnp.int32)

    new_k, new_v = [], []
    for l in range(DEPTH):
        lam_init = 0.8 - 0.6 * math.exp(-0.3 * l)
        q, k, v, zpre, gates = _inproj(
            x, mod[l], norm_mix[l].reshape(1, D_MODEL), w_in_bf, l,
            jnp.tile(q_norm[l], ATTN_WIDTH // HEAD_DIM).reshape(1, ATTN_WIDTH),
            jnp.tile(k_norm[l], ATTN_WIDTH // HEAD_DIM).reshape(1, ATTN_WIDTH),
            gsum, cos_t, sin_t, n_ctx_tiles, tiles_per_seq)
        new_k.append(k[:n_ctx].reshape(b_p, s_p, N_HEADS, 2 * HEAD_DIM))
        new_v.append(v[:n_ctx].reshape(b_p, s_p, N_HEADS, 2 * HEAD_DIM))
        sub_l = sub_norm[l].reshape(1, LANES)
        o_p = _attention(q, k, v, lam_qk[l], sub_l, lam_init, b_p, s_p, 0)
        o_s = _attention(q, k, v, lam_qk[l], sub_l, lam_init, b_s, s_s, n_ctx // s_s, ctx=(ck, cv, l))
        conv_args = (conv_w[l], conv_b[l].reshape(1, CONV_WIDTH), conv_norm_g[l].reshape(1, CONV_WIDTH),
                     conv_norm_b[l].reshape(1, CONV_WIDTH))
        zc_p = _conv_module(zpre, *conv_args, b_p, s_p, 0)
        zc_s = _conv_module(zpre, *conv_args, b_s, s_s, n_ctx // s_s)
        wr_pad = jnp.zeros((D_MODEL, LANES), F32).at[:, :N_EXPERTS].set(w_router[l])
        wr_hi = wr_pad.astype(BF16)
        wr_lo = (wr_pad - wr_hi.astype(F32)).astype(BF16)
        br_pad = jnp.full((1, LANES), NEG_BIG, F32).at[0, :N_EXPERTS].set(b_router[l])
        x1, h2, top_e, gatew, rank, counts = _outproj(
            x, o_p, o_s, zc_p, zc_s, gates, mod[l], wa_bf, wc_bf, wo_bf, l,
            norm_ffn[l].reshape(1, D_MODEL), wr_hi, wr_lo, br_pad, tri, s_s)

        cnt = counts[0, :N_EXPERTS]
        padded = (cnt + EBLK - 1) // EBLK * EBLK
        pend = jnp.cumsum(padded)
        pstart = pend - padded
        block_e = jnp.minimum(jnp.sum((pend[None, :] <= blk_row0[:, None]).astype(jnp.int32), axis=1),
                              N_EXPERTS - 1)
        n_used = (pend[-1] // EBLK).astype(jnp.int32).reshape(1)
        has = (cnt > 0).astype(jnp.int32)
        run_idx = jnp.cumsum(has) - has
        eid = jnp.arange(N_EXPERTS, dtype=jnp.int32)
        later = jnp.where((eid[None, :] > eid[:, None]) & (has[None, :] > 0), eid[None, :], N_EXPERTS)
        next_e = jnp.min(later, axis=1)
        next_e = jnp.where(next_e < N_EXPERTS, next_e, -1).astype(jnp.int32)
        run_par = (run_idx % 2).astype(jnp.int32)[block_e]
        run_next = next_e[block_e]

        dest = (pstart[top_e[:, :TOP_K]] + rank[:, :TOP_K]).reshape(n_tok * TOP_K)
        inv = _invert(dest, inv_init)
        yb = _experts(h2, inv, block_e, n_used, run_par, run_next, w_gate_up, b_gate_up, w_down, b_down, l)
        x = _combine(x1, mod[l], gatew, dest, yb, n_ctx_tiles, tiles_per_seq, split_out=(l == DEPTH - 1))

    y_prompt = x[0].reshape(b_p, s_p, D_MODEL)
    y_sample = x[1].reshape(b_s, s_s, D_MODEL)
    return (y_prompt, y_sample, jnp.stack(new_k, axis=1), jnp.stack(new_v, axis=1))
```
